```python
import jax
import jax.numpy as jnp
from jax import lax


D_MODEL = 1024
BATCH = 16
SEQ = 4096
DEPTH = 4
DEC_BATCH = 16
DEC_SEQ = 32
PAST_LEN = 4096

CHUNK = 64
N_META = 16
QBLOCK = 128
DSA_QBLOCK = 64
ROPE_THETA = 10000.0
NEG_INF = -1e30
LN_EPS = 1e-5
RMS_EPS = 1e-6
GN_EPS = 64e-5
ALPHA = (2 * DEPTH) ** 0.25
BETA = (8 * DEPTH) ** -0.25

H_A = D_MODEL // 256
Q_LORA = D_MODEL // 4
KV_LORA = D_MODEL // 8
NOPE_A = 64
ROPE_A = 32
V_A = 64

H_B = D_MODEL // 256
DH_B = 64
H_I = 8
D_I = 32
TOPK_MAX = 256

H_C = D_MODEL // 128
N_C = 64
W_C = H_C * N_C
W_LORA = 32
A_LORA = 32
G_LORA = 64

N_GROUPS = 4
EXP_PER_GROUP = 4
N_EXPERTS = N_GROUPS * EXP_PER_GROUP
D_EXPERT = 256
TOPK_IN_GROUP = 2

A_COLS = Q_LORA + KV_LORA + ROPE_A
B_COLS = 3 * H_B * DH_B + H_I * D_I + D_I + H_I
C_COLS = 3 * W_C + W_LORA + A_LORA + G_LORA
IN_COLS = A_COLS + B_COLS + C_COLS
MIX_WIDTH = H_A * V_A + H_B * DH_B + W_C

STATE_KEYS = ('ckv', 'krope', 'dsa_k', 'dsa_v', 'dsa_kidx', 'rwkv', 'shift')

kernel_name = 'hybrid_mla_dsa_rwkv7_hmoe_stream_step'


def _split(a, sizes):
    parts, off = [], 0
    for s in sizes:
        parts.append(a[..., off:off + s])
        off += s
    return parts


def _layernorm(x, g, b):
    xf = x.astype(jnp.float32)
    mu = jnp.mean(xf, -1, keepdims=True)
    var = jnp.mean(jnp.square(xf - mu), -1, keepdims=True)
    return ((xf - mu) * lax.rsqrt(var + LN_EPS) * g + b).astype(x.dtype)


def _rmsnorm(x, g):
    xf = x.astype(jnp.float32)
    return (xf * lax.rsqrt(jnp.mean(jnp.square(xf), -1, keepdims=True) + RMS_EPS) * g).astype(x.dtype)


def _rope(x, pos):
    half = x.shape[-1] // 2
    inv = ROPE_THETA ** (-jnp.arange(half, dtype=jnp.float32) / half)
    ang = pos.astype(jnp.float32)[:, None] * inv[None, :]
    cos = jnp.cos(ang)[None, :, None, :]
    sin = jnp.sin(ang)[None, :, None, :]
    xf = x.astype(jnp.float32)
    x1, x2 = xf[..., :half], xf[..., half:]
    return jnp.concatenate([x1 * cos - x2 * sin, x1 * sin + x2 * cos], -1).astype(x.dtype)


def _map_query_blocks(fn, block, batched, per_pos):
    lq = batched[0].shape[1]
    bs = min(block, lq)
    nb = -(-lq // bs)
    pad = nb * bs - lq

    def to_blocks(a, axis):
        widths = [(0, 0)] * a.ndim
        widths[axis] = (0, pad)
        a = jnp.pad(a, widths)
        a = a.reshape(a.shape[:axis] + (nb, bs) + a.shape[axis + 1:])
        return jnp.moveaxis(a, axis, 0)

    qs = tuple(to_blocks(a, 1) for a in batched)
    ps = tuple(to_blocks(a, 0) for a in per_pos)

    def body(args):
        qb, pb = args
        return fn(*qb, *pb)

    out = lax.map(body, (qs, ps))
    out = jnp.moveaxis(out, 0, 1)
    out = out.reshape((out.shape[0], nb * bs) + out.shape[3:])
    return out[:, :lq]


def _mla(za, pos_q, chunk_q, chunk_k, ckv_past, krope_past, q_norm, kv_norm, w_uq, w_ukv):
    B, L, _ = za.shape
    q_low, ckv, krope = _split(za, (Q_LORA, KV_LORA, ROPE_A))
    q = (_rmsnorm(q_low, q_norm) @ w_uq).reshape(B, L, H_A, NOPE_A + ROPE_A)
    q_nope = q[..., :NOPE_A]
    q_rope = _rope(q[..., NOPE_A:], pos_q)
    ckv_new = _rmsnorm(ckv, kv_norm)
    krope_new = _rope(krope[:, :, None, :], pos_q)[:, :, 0, :]
    ckv_all = jnp.concatenate([ckv_past, ckv_new], axis=1)
    krope_all = jnp.concatenate([krope_past, krope_new], axis=1)
    kv = (ckv_all @ w_ukv).reshape(B, -1, H_A, NOPE_A + V_A)
    k_nope, v = kv[..., :NOPE_A], kv[..., NOPE_A:]
    scale = (NOPE_A + ROPE_A) ** -0.5

    def block(qn, qr, cq):
        s = jnp.einsum('bqhd,bkhd->bhqk', qn, k_nope) + jnp.einsum('bqhd,bkd->bhqk', qr, krope_all)
        s = s.astype(jnp.float32) * scale
        vis = chunk_k[None, :] <= cq[:, None]
        s = jnp.where(vis, s, NEG_INF)
        p = jax.nn.softmax(s, axis=-1).astype(v.dtype)
        return jnp.einsum('bhqk,bkhd->bqhd', p, v)

    o = _map_query_blocks(block, QBLOCK, (q_nope, q_rope), (chunk_q,))
    return o.reshape(B, L, H_A * V_A), ckv_new, krope_new


def _dsa(zb, pos_q, chunk_q, chunk_k, k_past, v_past, ki_past, k_sel):
    B, L, _ = zb.shape
    hd = H_B * DH_B
    q, k, v, qi, ki, wi = _split(zb, (hd, hd, hd, H_I * D_I, D_I, H_I))
    q = _rope(q.reshape(B, L, H_B, DH_B), pos_q)
    k_new = _rope(k.reshape(B, L, H_B, DH_B), pos_q)
    v_new = v.reshape(B, L, H_B, DH_B)
    qi = _rope(qi.reshape(B, L, H_I, D_I), pos_q)
    ki_new = _rope(ki[:, :, None, :], pos_q)[:, :, 0, :]
    wi = wi.astype(jnp.float32) * (H_I ** -0.5)
    k_all = jnp.concatenate([k_past, k_new], axis=1)
    v_all = jnp.concatenate([v_past, v_new], axis=1)
    ki_all = jnp.concatenate([ki_past, ki_new], axis=1)

    def block(qb, qib, wib, cq):
        vis = chunk_k[None, :] <= cq[:, None]
        rel = jax.nn.relu(jnp.einsum('bqhd,bkd->bqhk', qib, ki_all).astype(jnp.float32))
        score = jnp.einsum('bqhk,bqh->bqk', rel, wib) * (D_I ** -0.5)
        score = jnp.where(vis[None], score, -jnp.inf)
        top_val, top_idx = lax.top_k(score, k_sel)
        keep = jnp.isfinite(top_val)
        k_rows = jax.vmap(lambda rows, idx: rows[idx])(k_all, top_idx)
        v_rows = jax.vmap(lambda rows, idx: rows[idx])(v_all, top_idx)
        s = jnp.einsum('bqhd,bqkhd->bqhk', qb, k_rows).astype(jnp.float32) * (DH_B ** -0.5)
        s = jnp.where(keep[:, :, None, :], s, NEG_INF)
        p = jax.nn.softmax(s, axis=-1).astype(v_rows.dtype)
        return jnp.einsum('bqhk,bqkhd->bqhd', p, v_rows)

    o = _map_query_blocks(block, DSA_QBLOCK, (q, qi, wi), (chunk_q,))
    return o.reshape(B, L, hd), k_new, v_new, ki_new


def _rwkv7(zc, shift_prev, state_prev, mu, w0, w_up, a0, a_up, g_up, k_k, k_a, r_k, ln_g, ln_b):
    B, L, _ = zc.shape
    f32 = jnp.float32
    z_prev = jnp.concatenate([shift_prev[:, None, :], zc[:, :-1]], axis=1)
    zs = zc + (z_prev - zc) * mu
    r, k, v, wl, al, gl = _split(zs, (W_C, W_C, W_C, W_LORA, A_LORA, G_LORA))
    w = -jax.nn.softplus(-(w0 + jnp.tanh(wl) @ w_up).astype(f32)) - 0.5
    decay = jnp.exp(-jnp.exp(w))
    a = jax.nn.sigmoid((a0 + al @ a_up).astype(f32))
    g = (jax.nn.sigmoid(gl) @ g_up).astype(f32)
    k_mod = k.astype(f32) * (1.0 + (a - 1.0) * k_a)

    def heads(t):
        return t.astype(f32).reshape(B, L, H_C, N_C)

    kk = heads(k * k_k)
    kk = kk / jnp.maximum(jnp.sqrt(jnp.sum(kk * kk, -1, keepdims=True)), 1e-12)
    r_h, k_h, v_h, d_h, a_h = heads(r), heads(k_mod), heads(v), heads(decay), heads(a)

    def step(S, inp):
        r_t, k_t, v_t, d_t, kk_t, a_t = inp
        sk = jnp.einsum('bhij,bhj->bhi', S, kk_t)
        S = (S * d_t[:, :, None, :] - sk[..., None] * (kk_t * a_t)[:, :, None, :]
             + v_t[..., None] * k_t[:, :, None, :])
        return S, jnp.einsum('bhij,bhj->bhi', S, r_t)

    xs = tuple(jnp.moveaxis(t, 1, 0) for t in (r_h, k_h, v_h, d_h, kk, a_h))
    S_last, ys = lax.scan(step, state_prev.astype(f32), xs)
    y = jnp.moveaxis(ys, 0, 1)
    ym = jnp.mean(y, -1, keepdims=True)
    yv = jnp.mean(jnp.square(y - ym), -1, keepdims=True)
    yn = ((y - ym) * lax.rsqrt(yv + GN_EPS)).reshape(B, L, W_C) * ln_g + ln_b
    bonus = jnp.sum(r_h * k_h * r_k.astype(f32), -1, keepdims=True) * v_h
    out = (yn + bonus.reshape(B, L, W_C)) * g
    return out.astype(zc.dtype), S_last.astype(zc.dtype), zc[:, -1]


def _moe(x, wg, bg, we, be, w_gate, w_up, w_down):
    B, L, D = x.shape
    f32 = jnp.float32
    xf = x.reshape(B * L, D)
    g_prob = jax.nn.softmax((xf @ wg).astype(f32) + bg.astype(f32), axis=-1)
    g_val, g_idx = lax.top_k(g_prob, 1)
    e_logit = ((xf @ we).astype(f32) + be.astype(f32)).reshape(-1, N_GROUPS, EXP_PER_GROUP)
    e_logit = jnp.take_along_axis(e_logit, g_idx[:, :, None], axis=1)[:, 0]
    e_val, e_idx = lax.top_k(jax.nn.softmax(e_logit, axis=-1), TOPK_IN_GROUP)
    gate = g_val * e_val / jnp.sum(e_val, -1, keepdims=True)
    expert_id = g_idx * EXP_PER_GROUP + e_idx
    dense_gate = jnp.einsum('tk,tke->te', gate, jax.nn.one_hot(expert_id, N_EXPERTS, dtype=f32))
    y = jnp.zeros((B * L, D), f32)
    for e in range(N_EXPERTS):
        h = jax.nn.silu(xf @ w_gate[e]) * (xf @ w_up[e])
        y = y + dense_gate[:, e:e + 1] * (h @ w_down[e]).astype(f32)
    return y.reshape(B, L, D).astype(x.dtype)


def _layer(x, pos_q, chunk_q, chunk_k, past, p, k_sel):
    z = x @ p['w_in']
    za, zb, zc = _split(z, (A_COLS, B_COLS, C_COLS))
    oa, ckv, krope = _mla(za, pos_q, chunk_q, chunk_k, past['ckv'], past['krope'],
                          p['mla_q_norm'], p['mla_kv_norm'], p['mla_w_uq'], p['mla_w_ukv'])
    ob, kb, vb, kib = _dsa(zb, pos_q, chunk_q, chunk_k, past['dsa_k'], past['dsa_v'], past['dsa_kidx'], k_sel)
    oc, s_new, shift_new = _rwkv7(zc, past['shift'], past['rwkv'], p['rwkv_mu'], p['rwkv_w0'], p['rwkv_w_up'],
                                  p['rwkv_a0'], p['rwkv_a_up'], p['rwkv_g_up'], p['rwkv_k_k'], p['rwkv_k_a'],
                                  p['rwkv_r_k'], p['rwkv_ln_g'], p['rwkv_ln_b'])
    mix = jnp.concatenate([oa, ob, oc], axis=-1) @ p['w_out']
    x = _layernorm(ALPHA * x + mix, p['ln1_g'], p['ln1_b'])
    ffn = _moe(x, p['router_group'], p['router_group_b'], p['router_expert'], p['router_expert_b'],
               p['exp_w_gate'], p['exp_w_up'], p['exp_w_down'])
    x = _layernorm(ALPHA * x + ffn, p['ln2_g'], p['ln2_b'])
    new = {'ckv': ckv, 'krope': krope, 'dsa_k': kb, 'dsa_v': vb, 'dsa_kidx': kib,
           'rwkv': s_new, 'shift': shift_new}
    return x, new


def _trunk(x, pos_q, chunk_q, chunk_k, past, weights, k_sel):
    new = {key: [] for key in STATE_KEYS}
    for layer in range(DEPTH):
        p = {name: arr[layer] for name, arr in weights.items()}
        pl = {key: past[key][layer] for key in STATE_KEYS}
        x, st = _layer(x, pos_q, chunk_q, chunk_k, pl, p, k_sel)
        for key in STATE_KEYS:
            new[key].append(st[key])
    return x, {key: jnp.stack(new[key]) for key in STATE_KEYS}


def setup_inputs(seed: int = 0) -> dict:
    key = jax.random.key(seed)
    ks = iter(jax.random.split(key, 48))
    f32 = jnp.float32

    def nrm(shape, s):
        return jax.random.normal(next(ks), shape, f32) * s

    d = {}
    d['x_prompt'] = nrm((BATCH, SEQ, D_MODEL), 1.0)
    d['x_sample'] = nrm((DEC_BATCH, DEC_SEQ, D_MODEL), 1.0)
    d['cache_mla_ckv'] = nrm((DEPTH, DEC_BATCH, PAST_LEN, KV_LORA), 1.0)
    d['cache_mla_krope'] = nrm((DEPTH, DEC_BATCH, PAST_LEN, ROPE_A), 1.0)
    d['cache_dsa_k'] = nrm((DEPTH, DEC_BATCH, PAST_LEN, H_B, DH_B), 1.0)
    d['cache_dsa_v'] = nrm((DEPTH, DEC_BATCH, PAST_LEN, H_B, DH_B), 1.0)
    d['cache_dsa_kidx'] = nrm((DEPTH, DEC_BATCH, PAST_LEN, D_I), 1.0)
    d['state_rwkv'] = nrm((DEPTH, DEC_BATCH, H_C, N_C, N_C), 0.3)
    d['state_rwkv_shift'] = nrm((DEPTH, DEC_BATCH, C_COLS), 1.0)
    d['meta_tokens'] = nrm((N_META, D_MODEL), 1.0)
    d['w_in'] = nrm((DEPTH, D_MODEL, IN_COLS), D_MODEL ** -0.5)
    d['mla_q_norm'] = 1.0 + nrm((DEPTH, Q_LORA), 0.02)
    d['mla_kv_norm'] = 1.0 + nrm((DEPTH, KV_LORA), 0.02)
    d['mla_w_uq'] = nrm((DEPTH, Q_LORA, H_A * (NOPE_A + ROPE_A)), Q_LORA ** -0.5)
    d['mla_w_ukv'] = nrm((DEPTH, KV_LORA, H_A * (NOPE_A + V_A)), KV_LORA ** -0.5)
    d['rwkv_mu'] = jax.random.uniform(next(ks), (DEPTH, C_COLS), f32)
    d['rwkv_w0'] = nrm((DEPTH, W_C), 0.5) - 0.5
    d['rwkv_w_up'] = nrm((DEPTH, W_LORA, W_C), 0.1)
    d['rwkv_a0'] = nrm((DEPTH, W_C), 0.1)
    d['rwkv_a_up'] = nrm((DEPTH, A_LORA, W_C), 0.1)
    d['rwkv_g_up'] = nrm((DEPTH, G_LORA, W_C), G_LORA ** -0.5)
    d['rwkv_k_k'] = 0.85 + nrm((DEPTH, W_C), 0.02)
    d['rwkv_k_a'] = 1.0 + nrm((DEPTH, W_C), 0.02)
    d['rwkv_r_k'] = nrm((DEPTH, H_C, N_C), 0.1)
    d['rwkv_ln_g'] = 1.0 + nrm((DEPTH, W_C), 0.02)
    d['rwkv_ln_b'] = nrm((DEPTH, W_C), 0.02)
    d['w_out'] = nrm((DEPTH, MIX_WIDTH, D_MODEL), BETA * MIX_WIDTH ** -0.5)
    d['ln1_g'] = 1.0 + nrm((DEPTH, D_MODEL), 0.02)
    d['ln1_b'] = nrm((DEPTH, D_MODEL), 0.02)
    d['ln2_g'] = 1.0 + nrm((DEPTH, D_MODEL), 0.02)
    d['ln2_b'] = nrm((DEPTH, D_MODEL), 0.02)
    d['router_group'] = nrm((DEPTH, D_MODEL, N_GROUPS), D_MODEL ** -0.5)
    d['router_group_b'] = nrm((DEPTH, N_GROUPS), 0.01)
    d['router_expert'] = nrm((DEPTH, D_MODEL, N_EXPERTS), D_MODEL ** -0.5)
    d['router_expert_b'] = nrm((DEPTH, N_EXPERTS), 0.01)
    d['exp_w_gate'] = nrm((DEPTH, N_EXPERTS, D_MODEL, D_EXPERT), D_MODEL ** -0.5)
    d['exp_w_up'] = nrm((DEPTH, N_EXPERTS, D_MODEL, D_EXPERT), D_MODEL ** -0.5)
    d['exp_w_down'] = nrm((DEPTH, N_EXPERTS, D_EXPERT, D_MODEL), BETA * D_EXPERT ** -0.5)
    return d


def reference(x_prompt, x_sample, cache_mla_ckv, cache_mla_krope, cache_dsa_k, cache_dsa_v, cache_dsa_kidx,
              state_rwkv, state_rwkv_shift, meta_tokens, w_in, mla_q_norm, mla_kv_norm, mla_w_uq, mla_w_ukv,
              rwkv_mu, rwkv_w0, rwkv_w_up, rwkv_a0, rwkv_a_up, rwkv_g_up, rwkv_k_k, rwkv_k_a, rwkv_r_k,
              rwkv_ln_g, rwkv_ln_b, w_out, ln1_g, ln1_b, ln2_g, ln2_b, router_group, router_group_b,
              router_expert, router_expert_b, exp_w_gate, exp_w_up, exp_w_down):
    weights = {
        'w_in': w_in, 'mla_q_norm': mla_q_norm, 'mla_kv_norm': mla_kv_norm, 'mla_w_uq': mla_w_uq,
        'mla_w_ukv': mla_w_ukv, 'rwkv_mu': rwkv_mu, 'rwkv_w0': rwkv_w0, 'rwkv_w_up': rwkv_w_up,
        'rwkv_a0': rwkv_a0, 'rwkv_a_up': rwkv_a_up, 'rwkv_g_up': rwkv_g_up, 'rwkv_k_k': rwkv_k_k,
        'rwkv_k_a': rwkv_k_a, 'rwkv_r_k': rwkv_r_k, 'rwkv_ln_g': rwkv_ln_g, 'rwkv_ln_b': rwkv_ln_b,
        'w_out': w_out, 'ln1_g': ln1_g, 'ln1_b': ln1_b, 'ln2_g': ln2_g, 'ln2_b': ln2_b,
        'router_group': router_group, 'router_group_b': router_group_b, 'router_expert': router_expert,
        'router_expert_b': router_expert_b, 'exp_w_gate': exp_w_gate, 'exp_w_up': exp_w_up,
        'exp_w_down': exp_w_down,
    }

    bp, dt = x_prompt.shape[0], x_prompt.dtype
    meta = jnp.broadcast_to(meta_tokens.astype(dt)[None], (bp, N_META, D_MODEL))
    xp = jnp.concatenate([meta, x_prompt], axis=1)
    lp = xp.shape[1]
    pos_p = jnp.arange(lp, dtype=jnp.int32)
    chunk_p = jnp.where(pos_p < N_META, -1, (pos_p - N_META) // CHUNK)
    past_p = {
        'ckv': jnp.zeros((DEPTH, bp, 0, KV_LORA), dt),
        'krope': jnp.zeros((DEPTH, bp, 0, ROPE_A), dt),
        'dsa_k': jnp.zeros((DEPTH, bp, 0, H_B, DH_B), dt),
        'dsa_v': jnp.zeros((DEPTH, bp, 0, H_B, DH_B), dt),
        'dsa_kidx': jnp.zeros((DEPTH, bp, 0, D_I), dt),
        'rwkv': jnp.zeros((DEPTH, bp, H_C, N_C, N_C), dt),
        'shift': jnp.zeros((DEPTH, bp, C_COLS), dt),
    }
    yp, new_p = _trunk(xp, pos_p, chunk_p, chunk_p, past_p, weights, min(TOPK_MAX, SEQ // 4))
    y_prompt = yp[:, N_META:]

    n_past, n_new = cache_mla_ckv.shape[2], x_sample.shape[1]
    pos_s = n_past + jnp.arange(n_new, dtype=jnp.int32)
    chunk_s = pos_s // CHUNK
    chunk_ks = jnp.concatenate([jnp.arange(n_past, dtype=jnp.int32) // CHUNK, chunk_s])
    past_s = {
        'ckv': cache_mla_ckv, 'krope': cache_mla_krope, 'dsa_k': cache_dsa_k, 'dsa_v': cache_dsa_v,
        'dsa_kidx': cache_dsa_kidx, 'rwkv': state_rwkv, 'shift': state_rwkv_shift,
    }
    y_sample, new_s = _trunk(x_sample, pos_s, chunk_s, chunk_ks, past_s, weights,
                             min(TOPK_MAX, (PAST_LEN + DEC_SEQ) // 4))

    return (y_prompt, y_sample,
            new_p['ckv'], new_p['krope'], new_p['dsa_k'], new_p['dsa_v'], new_p['dsa_kidx'],
            new_p['rwkv'], new_p['shift'],
            new_s['ckv'], new_s['krope'], new_s['dsa_k'], new_s['dsa_v'], new_s['dsa_kidx'],
            new_s['rwkv'], new_s['shift'])
```

```python
import functools

import jax
import jax.numpy as jnp
from jax import lax
from jax.experimental import pallas as pl
from jax.experimental.pallas import tpu as pltpu

F32 = jnp.float32
BF16 = jnp.bfloat16
I32 = jnp.int32

D_MODEL = 1024
DEPTH = 4
CHUNK = 64
N_META = 16
ROPE_THETA = 10000.0
NEG_INF = -1e30
LN_EPS = 1e-5
RMS_EPS = 1e-6
GN_EPS = 64e-5
ALPHA = (2 * DEPTH) ** 0.25

H_A, Q_LORA, KV_LORA, NOPE_A, ROPE_A, V_A = 4, 256, 128, 64, 32, 64
H_B, DH_B, H_I, D_I, TOPK_MAX = 4, 64, 8, 32, 256
H_C, N_C, W_C, W_LORA, A_LORA, G_LORA = 8, 64, 512, 32, 32, 64
N_GROUPS, EXP_PER_GROUP, N_EXPERTS, D_EXPERT = 4, 4, 16, 256

A_COLS = Q_LORA + KV_LORA + ROPE_A
B_COLS = 3 * H_B * DH_B + H_I * D_I + D_I + H_I
C_COLS = 3 * W_C + W_LORA + A_LORA + G_LORA

MLA_SCALE = (NOPE_A + ROPE_A) ** -0.5
DSA_SCALE = DH_B ** -0.5
IDX_SCALE = (H_I ** -0.5) * (D_I ** -0.5)

LANE = 128
HEAD_W = 128
A_W = Q_LORA + KV_LORA + LANE
B_Q, B_K, B_V = 0, H_B * HEAD_W, 2 * H_B * HEAD_W
B_QI = 3 * H_B * HEAD_W
B_KI = B_QI + H_I * D_I
B_W = B_KI + LANE
C_W = C_COLS
FRONT_PAD = CHUNK - N_META
KEY_TILE = 256
INT_MIN = -2147483648
INT_MAX = 2147483647
VMEM_LIMIT = 56 * 1024 * 1024


def _cparams(*sem):
    return pltpu.CompilerParams(dimension_semantics=sem, vmem_limit_bytes=VMEM_LIMIT)


def _full(shape):
    nd = len(shape)
    return pl.BlockSpec(shape, lambda *_: (0,) * nd)


def _tile_lanes(t, reps):
    return t if reps == 1 else jnp.concatenate([t] * reps, axis=1)


def _rope(x, tab_ref, half):
    n = x.shape[1]
    reps = n // LANE
    cos = _tile_lanes(tab_ref[0], reps)
    sin_lo = _tile_lanes(tab_ref[1], reps)
    sin_hi = _tile_lanes(tab_ref[2], reps)
    return x * cos + pltpu.roll(x, n - half, 1) * sin_lo + pltpu.roll(x, half, 1) * sin_hi


def _mm(a, b):
    if b.dtype == F32:
        return jnp.dot(a.astype(F32), b, preferred_element_type=F32, precision=lax.Precision.HIGHEST)
    return jnp.dot(a.astype(BF16), b, preferred_element_type=F32)


def _mm_t(a, b):
    prec = lax.Precision.HIGHEST if b.dtype == F32 else None
    return lax.dot_general(a.astype(b.dtype), b, (((1,), (1,)), ((), ())), preferred_element_type=F32,
                           precision=prec)


def _pair_heads(parts):
    lo = parts[0] + pltpu.roll(parts[1], 64, 1)
    hi = parts[2] + pltpu.roll(parts[3], 64, 1)
    return jnp.concatenate([lo, hi], axis=1)


def _proj_in_kernel(x_ref, wa_ref, wb_ref, wc_ref, za_ref, zb_ref, zc_ref):
    xb = x_ref[...].astype(wa_ref.dtype)
    za_ref[...] = _mm(xb, wa_ref[...])
    zb_ref[...] = _mm(xb, wb_ref[...])
    zc_ref[...] = _mm(xb, wc_ref[...])


def _proj_in(x2d, wa, wb, wc, tm):
    t = x2d.shape[0]
    row = lambda w: pl.BlockSpec((tm, w), lambda i: (i, 0))
    return pl.pallas_call(
        _proj_in_kernel, grid=(t // tm,),
        in_specs=[row(D_MODEL), _full(wa.shape), _full(wb.shape), _full(wc.shape)],
        out_specs=[row(A_W), row(B_W), row(C_W)],
        out_shape=[jax.ShapeDtypeStruct((t, w), F32) for w in (A_W, B_W, C_W)],
        compiler_params=_cparams("parallel"), name="proj_in")(x2d, wa, wb, wc)


def _rms(x, g):
    return x * lax.rsqrt(jnp.mean(x * x, axis=-1, keepdims=True) + RMS_EPS) * g


def _mla_prep_kernel(za_ref, tq_ref, tk_ref, qg_ref, kg_ref, wuq_ref, wk_ref, wv_ref,
                     q_ref, kc_ref, v_ref, ckv_ref, kr_ref):
    za = za_ref[...]
    qn = _rms(za[:, :Q_LORA], qg_ref[...])
    q = _rope(_mm(qn, wuq_ref[...]), tq_ref, ROPE_A // 2) * MLA_SCALE
    ckvn = _rms(za[:, Q_LORA:Q_LORA + KV_LORA], kg_ref[...])
    ckv_ref[...] = ckvn
    kr = _rope(za[:, Q_LORA + KV_LORA:], tk_ref, ROPE_A // 2)
    kr_ref[...] = kr[:, :ROPE_A]
    kc = _mm(ckvn, wk_ref[...]) + _tile_lanes(pltpu.roll(kr, NOPE_A, 1), H_A)
    v = _mm(ckvn, wv_ref[...])
    for h in range(H_A):
        sl = slice(h * HEAD_W, (h + 1) * HEAD_W)
        q_ref[h] = q[:, sl].astype(q_ref.dtype)
        kc_ref[h] = kc[:, sl].astype(kc_ref.dtype)
        v_ref[h] = v[:, sl].astype(v_ref.dtype)


def _mla_prep(za3, tq, tk, qg, kg, wuq, wk, wv, tl):
    b, l, _ = za3.shape
    heads = pl.BlockSpec((None, H_A, tl, HEAD_W), lambda bi, i: (bi, 0, i, 0))
    tab = pl.BlockSpec((3, tl, LANE), lambda bi, i: (0, i, 0))
    hshape = jax.ShapeDtypeStruct((b, H_A, l, HEAD_W), wuq.dtype)
    return pl.pallas_call(
        _mla_prep_kernel, grid=(b, l // tl),
        in_specs=[pl.BlockSpec((None, tl, A_W), lambda bi, i: (bi, i, 0)), tab, tab,
                  _full(qg.shape), _full(kg.shape), _full(wuq.shape), _full(wk.shape), _full(wv.shape)],
        out_specs=[heads, heads, heads,
                   pl.BlockSpec((None, tl, KV_LORA), lambda bi, i: (bi, i, 0)),
                   pl.BlockSpec((None, tl, ROPE_A), lambda bi, i: (bi, i, 0))],
        out_shape=[hshape, hshape, hshape,
                   jax.ShapeDtypeStruct((b, l, KV_LORA), F32), jax.ShapeDtypeStruct((b, l, ROPE_A), F32)],
        compiler_params=_cparams("parallel", "parallel"), name="mla_prep")(za3, tq, tk, qg, kg, wuq, wk, wv)


def _mla_past_kernel(ckv_ref, kr_ref, wk_ref, wv_ref, kc_ref, v_ref):
    cb = ckv_ref[...]
    kc = _mm(cb, wk_ref[...]) + _tile_lanes(pltpu.roll(kr_ref[...], NOPE_A, 1), H_A)
    v = _mm(cb, wv_ref[...])
    for h in range(H_A):
        sl = slice(h * HEAD_W, (h + 1) * HEAD_W)
        kc_ref[h] = kc[:, sl].astype(kc_ref.dtype)
        v_ref[h] = v[:, sl].astype(v_ref.dtype)


def _mla_past(ckv, kr128, wk, wv, tl):
    b, p, _ = ckv.shape
    heads = pl.BlockSpec((None, H_A, tl, HEAD_W), lambda bi, i: (bi, 0, i, 0))
    hshape = jax.ShapeDtypeStruct((b, H_A, p, HEAD_W), wk.dtype)
    return pl.pallas_call(
        _mla_past_kernel, grid=(b, p // tl),
        in_specs=[pl.BlockSpec((None, tl, KV_LORA), lambda bi, i: (bi, i, 0)),
                  pl.BlockSpec((None, tl, LANE), lambda bi, i: (bi, i, 0)), _full(wk.shape), _full(wv.shape)],
        out_specs=[heads, heads], out_shape=[hshape, hshape],
        compiler_params=_cparams("parallel", "parallel"), name="mla_past")(ckv, kr128, wk, wv)


def _softmax_step(qh, k_t, v_t, mask, carry):
    m, l, acc = carry
    s = _mm_t(qh, k_t)
    if mask is not None:
        s = jnp.where(mask, s, NEG_INF)
    m_new = jnp.maximum(m, jnp.max(s, axis=1, keepdims=True))
    alpha = jnp.exp(m - m_new)
    p = jnp.exp(s - m_new)
    l = alpha * l + jnp.sum(p, axis=1, keepdims=True)
    acc = alpha * acc + _mm(p, v_t)
    return m_new, l, acc


def _softmax_init(tq):
    return (jnp.full((tq, 1), NEG_INF, F32), jnp.zeros((tq, 1), F32), jnp.zeros((tq, HEAD_W), F32))


def _mla_attn_kernel(*refs, tq, tkn, ln, causal, first_key, p_tiles, tkp):
    if p_tiles:
        q_ref, kp_ref, vp_ref, kn_ref, vn_ref, o_ref = refs
    else:
        q_ref, kn_ref, vn_ref, o_ref = refs
    qt = pl.program_id(1)
    if causal:
        row = qt * tq + lax.broadcasted_iota(I32, (tq, 1), 0)
        nvis = ((row >> 6) + 1) << 6
        n_tiles = (qt * tq + tq + tkn - 1) // tkn
    else:
        n_tiles = ln // tkn
    outs = []
    for h in range(H_A):
        qh = q_ref[h]
        carry = _softmax_init(tq)
        if p_tiles:
            def past_body(j, c, h=h, qh=qh):
                off = pl.multiple_of(j * tkp, tkp)
                return _softmax_step(qh, kp_ref[h, pl.ds(off, tkp), :], vp_ref[h, pl.ds(off, tkp), :], None, c)
            carry = lax.fori_loop(0, p_tiles, past_body, carry)

        def new_body(j, c, h=h, qh=qh):
            off = pl.multiple_of(j * tkn, tkn)
            mask = None
            if causal:
                col = off + lax.broadcasted_iota(I32, (1, tkn), 1)
                mask = (col >= first_key) & (col < nvis)
            return _softmax_step(qh, kn_ref[h, pl.ds(off, tkn), :], vn_ref[h, pl.ds(off, tkn), :], mask, c)
        _, l, acc = lax.fori_loop(0, n_tiles, new_body, carry)
        outs.append(acc / l)
    o_ref[...] = _pair_heads(outs)


def _mla_attn(q, kn, vn, past, tq, causal, first_key):
    b, _, l, _ = q.shape
    ln = kn.shape[2]
    tkn = KEY_TILE if ln % KEY_TILE == 0 else ln
    qspec = pl.BlockSpec((None, H_A, tq, HEAD_W), lambda bi, i: (bi, 0, i, 0))
    whole = lambda n: pl.BlockSpec((None, H_A, n, HEAD_W), lambda bi, i: (bi, 0, 0, 0))
    args, specs, p_tiles = [q], [qspec], 0
    if past is not None:
        p = past[0].shape[2]
        p_tiles = p // KEY_TILE
        args += list(past)
        specs += [whole(p), whole(p)]
    args += [kn, vn]
    specs += [whole(ln), whole(ln)]
    kern = functools.partial(_mla_attn_kernel, tq=tq, tkn=tkn, ln=ln, causal=causal, first_key=first_key,
                             p_tiles=p_tiles, tkp=KEY_TILE)
    return pl.pallas_call(
        kern, grid=(b, l // tq), in_specs=specs,
        out_specs=pl.BlockSpec((None, tq, H_A * V_A), lambda bi, i: (bi, i, 0)),
        out_shape=jax.ShapeDtypeStruct((b, l, H_A * V_A), F32),
        compiler_params=_cparams("parallel", "parallel"), name="mla_attn")(*args)


def _dsa_prep_kernel(zb_ref, tqk_ref, tiq_ref, tik_ref,
                     q_ref, k_ref, v_ref, qi_ref, ki_ref, wi_ref, kst_ref, vst_ref, kist_ref):
    zb = zb_ref[...]
    q = _rope(zb[:, B_Q:B_K], tqk_ref, DH_B // 2) * DSA_SCALE
    k = _rope(zb[:, B_K:B_V], tqk_ref, DH_B // 2)
    v = zb[:, B_V:B_QI]
    qi = _rope(zb[:, B_QI:B_KI], tiq_ref, D_I // 2)
    kiw = _rope(zb[:, B_KI:B_W], tik_ref, D_I // 2)
    ks, vs = [], []
    for h in range(H_B):
        sl = slice(h * HEAD_W, (h + 1) * HEAD_W)
        q_ref[h] = q[:, sl].astype(q_ref.dtype)
        k_ref[h] = k[:, sl].astype(k_ref.dtype)
        v_ref[h] = v[:, sl].astype(v_ref.dtype)
        ks.append(k[:, sl])
        vs.append(v[:, sl])
    for h in range(H_I):
        qi_ref[h] = qi[:, h * D_I:(h + 1) * D_I].astype(qi_ref.dtype)
    ki_ref[...] = kiw[:, :D_I].astype(ki_ref.dtype)
    wi_ref[...] = kiw * IDX_SCALE
    kst_ref[...] = _pair_heads(ks)
    vst_ref[...] = _pair_heads(vs)
    kist_ref[...] = kiw[:, :D_I]


def _dsa_prep(zb3, tqk, tiq, tik, tl, mm_dtype):
    b, l, _ = zb3.shape
    heads = pl.BlockSpec((None, H_B, tl, HEAD_W), lambda bi, i: (bi, 0, i, 0))
    tab = pl.BlockSpec((3, tl, LANE), lambda bi, i: (0, i, 0))
    rows = lambda w: pl.BlockSpec((None, tl, w), lambda bi, i: (bi, i, 0))
    hshape = jax.ShapeDtypeStruct((b, H_B, l, HEAD_W), mm_dtype)
    return pl.pallas_call(
        _dsa_prep_kernel, grid=(b, l // tl),
        in_specs=[rows(B_W), tab, tab, tab],
        out_specs=[heads, heads, heads,
                   pl.BlockSpec((None, H_I, tl, D_I), lambda bi, i: (bi, 0, i, 0)),
                   rows(D_I), rows(LANE), rows(H_B * DH_B), rows(H_B * DH_B), rows(D_I)],
        out_shape=[hshape, hshape, hshape,
                   jax.ShapeDtypeStruct((b, H_I, l, D_I), mm_dtype),
                   jax.ShapeDtypeStruct((b, l, D_I), mm_dtype),
                   jax.ShapeDtypeStruct((b, l, LANE), F32),
                   jax.ShapeDtypeStruct((b, l, H_B * DH_B), F32),
                   jax.ShapeDtypeStruct((b, l, H_B * DH_B), F32),
                   jax.ShapeDtypeStruct((b, l, D_I), F32)],
        compiler_params=_cparams("parallel", "parallel"), name="dsa_prep")(zb3, tqk, tiq, tik)


def _order_key(x):
    x = jnp.where(x == 0.0, 0.0, x)
    b = pltpu.bitcast(x, I32)
    return jnp.where(b < 0, b ^ INT_MAX, b)


def _dsa_attn_kernel(*refs, tq, tkn, ln, causal, first_key, p_len, tkp, k_sel):
    if p_len:
        (q_ref, qi_ref, wi_ref, kp_ref, vp_ref, kip_ref, kn_ref, vn_ref, kin_ref, o_ref, keys_ref) = refs
    else:
        (q_ref, qi_ref, wi_ref, kn_ref, vn_ref, kin_ref, o_ref, keys_ref) = refs
    qt = pl.program_id(1)
    p_tiles = p_len // tkp if p_len else 0
    if causal:
        nvis = (qt + 1) * tq
        n_tiles = (nvis + tkn - 1) // tkn
        n_valid = nvis - first_key + p_len
    else:
        nvis = ln
        n_tiles = ln // tkn
        n_valid = ln + p_len

    sources = []
    if p_len:
        sources.append((kp_ref, vp_ref, kip_ref, 0, tkp, p_tiles, False))
    sources.append((kn_ref, vn_ref, kin_ref, p_len, tkn, n_tiles, causal))

    def col_mask(off, tk):
        col = off + lax.broadcasted_iota(I32, (1, tk), 1)
        return (col >= first_key) & (col < nvis)

    q8 = qi_ref[...].reshape(H_I * tq, D_I)
    wi = wi_ref[...]
    w8 = jnp.concatenate([wi[:, D_I + h:D_I + h + 1] for h in range(H_I)], axis=0)
    for (_, _, ki_ref, base, tk, nt, masked) in sources:
        def score_body(j, c, ki_ref=ki_ref, base=base, tk=tk, masked=masked):
            off = pl.multiple_of(j * tk, tk)
            rel = jnp.maximum(_mm_t(q8, ki_ref[pl.ds(off, tk), :]), 0.0) * w8
            sc = rel[0:tq]
            for h in range(1, H_I):
                sc = sc + rel[h * tq:(h + 1) * tq]
            key = _order_key(sc)
            if masked:
                key = jnp.where(col_mask(off, tk), key, INT_MIN)
            keys_ref[:, pl.ds(pl.multiple_of(base + off, tk), tk)] = key
            return c
        lax.fori_loop(0, nt, score_body, 0)

    def count_ge(mid):
        total = jnp.zeros((tq, 1), F32)
        for (_, _, _, base, tk, nt, _) in sources:
            def cnt_body(j, acc, base=base, tk=tk):
                off = pl.multiple_of(base + j * tk, tk)
                return acc + jnp.where(keys_ref[:, pl.ds(off, tk)] >= mid, 1.0, 0.0)
            acc = lax.fori_loop(0, nt, cnt_body, jnp.zeros((tq, tk), F32))
            total = total + jnp.sum(acc, axis=1, keepdims=True)
        return total

    def bisect_body(_, c):
        lo, hi = c
        mid = (lo >> 1) + (hi >> 1) + (((lo & 1) + (hi & 1) + 1) >> 1)
        ge = count_ge(mid) >= float(k_sel)
        return jnp.where(ge, mid, lo), jnp.where(ge, hi, mid - 1)

    n_iter = jnp.where(n_valid > k_sel, 32, 0)
    thr, _ = lax.fori_loop(0, n_iter, bisect_body,
                           (jnp.full((tq, 1), INT_MIN + 1, I32), jnp.full((tq, 1), INT_MAX, I32)))

    outs = []
    for h in range(H_B):
        qh = q_ref[h]
        carry = _softmax_init(tq)
        for (k_ref, v_ref, _, base, tk, nt, _) in sources:
            def att_body(j, c, k_ref=k_ref, v_ref=v_ref, base=base, tk=tk, h=h, qh=qh):
                off = pl.multiple_of(j * tk, tk)
                sel = keys_ref[:, pl.ds(pl.multiple_of(base + off, tk), tk)] >= thr
                return _softmax_step(qh, k_ref[h, pl.ds(off, tk), :], v_ref[h, pl.ds(off, tk), :], sel, c)
            carry = lax.fori_loop(0, nt, att_body, carry)
        _, l, acc = carry
        outs.append(acc / l)
    o_ref[...] = _pair_heads(outs)


def _dsa_attn(q, qi, wi, kn, vn, kin, past, tq, causal, first_key, k_sel):
    b, _, l, _ = q.shape
    ln = kn.shape[2]
    tkn = KEY_TILE if ln % KEY_TILE == 0 else ln
    qspec = pl.BlockSpec((None, H_B, tq, HEAD_W), lambda bi, i: (bi, 0, i, 0))
    whole = lambda n: pl.BlockSpec((None, H_B, n, HEAD_W), lambda bi, i: (bi, 0, 0, 0))
    whole_ki = lambda n: pl.BlockSpec((None, n, D_I), lambda bi, i: (bi, 0, 0))
    args = [q, qi, wi]
    specs = [qspec, pl.BlockSpec((None, H_I, tq, D_I), lambda bi, i: (bi, 0, i, 0)),
             pl.BlockSpec((None, tq, LANE), lambda bi, i: (bi, i, 0))]
    p_len = 0
    if past is not None:
        p_len = past[0].shape[2]
        args += list(past)
        specs += [whole(p_len), whole(p_len), whole_ki(p_len)]
    args += [kn, vn, kin]
    specs += [whole(ln), whole(ln), whole_ki(ln)]
    kern = functools.partial(_dsa_attn_kernel, tq=tq, tkn=tkn, ln=ln, causal=causal, first_key=first_key,
                             p_len=p_len, tkp=KEY_TILE, k_sel=k_sel)
    return pl.pallas_call(
        kern, grid=(b, l // tq), in_specs=specs,
        out_specs=pl.BlockSpec((None, tq, H_B * DH_B), lambda bi, i: (bi, i, 0)),
        out_shape=jax.ShapeDtypeStruct((b, l, H_B * DH_B), F32),
        scratch_shapes=[pltpu.VMEM((tq, p_len + ln), I32)],
        compiler_params=_cparams("parallel", "arbitrary"), name="dsa_attn")(*args)


def _softplus(x):
    return jnp.maximum(x, 0.0) + jnp.log(1.0 + jnp.exp(-jnp.abs(x)))


def _rwkv_prep_kernel(zc_ref, zp_ref, sh_ref, mu_ref, w0_ref, a0_ref, kkw_ref, ka_ref,
                      wup_ref, aup_ref, gup_ref, bd_ref,
                      r_ref, k_ref, v_ref, ld_ref, kk_ref, b_ref, g_ref, *, tl, lo, hi):
    i = pl.program_id(1)
    z = zc_ref[...]
    rowi = lax.broadcasted_iota(I32, (tl, 1), 0)
    prev_last = jnp.where(i == 0, sh_ref[...], zp_ref[7:8, :])
    zprev = jnp.where(rowi == 0, prev_last, pltpu.roll(z, 1, 0))
    zs = z + (zprev - z) * mu_ref[...]
    r, k, v, lora = zs[:, :W_C], zs[:, W_C:2 * W_C], zs[:, 2 * W_C:3 * W_C], zs[:, 3 * W_C:]
    lane = lax.broadcasted_iota(I32, (1, LANE), 1)
    u = jnp.where(lane < W_LORA, jnp.tanh(lora),
                  jnp.where(lane < W_LORA + A_LORA, lora, jax.nn.sigmoid(lora)))
    w = -_softplus(-(w0_ref[...] + _mm(u, wup_ref[...]))) - 0.5
    ld = -jnp.exp(w)
    a = jax.nn.sigmoid(a0_ref[...] + _mm(u, aup_ref[...]))
    g = _mm(u, gup_ref[...])
    kmod = k * (1.0 + (a - 1.0) * ka_ref[...])
    kk = k * kkw_ref[...]
    sq = kk * kk
    bd = bd_ref[...]
    if bd.dtype == F32:
        ss = _mm(sq, bd)
    else:
        sq_hi = sq.astype(BF16)
        ss = _mm(sq_hi, bd) + _mm(sq - sq_hi.astype(F32), bd)
    kkn = kk * jnp.minimum(lax.rsqrt(ss), 1e12)
    grow = i * tl + rowi
    valid = ((grow >= lo) & (grow < hi)).astype(F32)
    r_ref[...] = r
    k_ref[...] = kmod * valid
    v_ref[...] = v * valid
    ld_ref[...] = ld * valid
    kk_ref[...] = kkn * valid
    b_ref[...] = kkn * a * valid
    g_ref[...] = g


def _rwkv_prep(zc3, shift, p, tl, lo, hi):
    b, l, _ = zc3.shape
    rows = lambda w: pl.BlockSpec((None, tl, w), lambda bi, i: (bi, i, 0))
    vec = _full((1, W_C))
    mat = _full((LANE, W_C))
    kern = functools.partial(_rwkv_prep_kernel, tl=tl, lo=lo, hi=hi)
    return pl.pallas_call(
        kern, grid=(b, l // tl),
        in_specs=[rows(C_W),
                  pl.BlockSpec((None, 8, C_W), lambda bi, i: (bi, jnp.maximum(i * (tl // 8) - 1, 0), 0)),
                  pl.BlockSpec((None, 1, C_W), lambda bi, i: (bi, 0, 0)),
                  _full((1, C_W)), vec, vec, vec, vec, mat, mat, mat, _full((W_C, W_C))],
        out_specs=[rows(W_C)] * 7,
        out_shape=[jax.ShapeDtypeStruct((b, l, W_C), F32)] * 7,
        compiler_params=_cparams("parallel", "parallel"), name="rwkv_prep")(
            zc3, zc3, shift, p['mu'], p['w0'], p['a0'], p['k_k'], p['k_a'], p['wup'], p['aup'], p['gup'], p['bd'])


def _rwkv_scan_kernel(r_ref, k_ref, v_ref, ld_ref, kk_ref, b_ref, g_ref, s0_ref, lng_ref, lnb_ref, rk_ref,
                      o_ref, sout_ref, s_scr, *, c, n_chunks, md):
    ci = pl.program_id(1)

    @pl.when(ci == 0)
    def _():
        s_scr[...] = s0_ref[...]

    ld = ld_ref[...]
    rowi = lax.broadcasted_iota(I32, (c, 1), 0)
    cum = ld
    step = 1
    while step < c:
        cum = cum + jnp.where(rowi >= step, pltpu.roll(cum, step, 0), 0.0)
        step *= 2
    cl = cum[c - 1:c, :]
    e_in, e_ex, e_neg, e_end = jnp.exp(cum), jnp.exp(cum - ld), jnp.exp(-cum), jnp.exp(cl - cum)
    d_end = jnp.exp(cl)
    r, k, v, kk, bb = r_ref[...], k_ref[...], v_ref[...], kk_ref[...], b_ref[...]
    rt, kap, kt, bt = r * e_in, kk * e_ex, k * e_neg, bb * e_neg
    kte, bte = k * e_end, bb * e_end
    bonus_rk = r * k * rk_ref[...]
    ri = lax.broadcasted_iota(I32, (c, c), 0)
    cj = lax.broadcasted_iota(I32, (c, c), 1)
    strict, incl = ri > cj, ri >= cj
    eye = jnp.where(ri == cj, 1.0, 0.0)
    n_sq = c.bit_length() - 2
    for h in range(H_C):
        sl = slice(h * N_C, (h + 1) * N_C)
        s_old = s_scr[h]
        s_mm = s_old.astype(md)
        left = jnp.concatenate([kap[:, sl], rt[:, sl]], axis=0).astype(md)
        right = jnp.concatenate([bt[:, sl], kt[:, sl]], axis=0).astype(md)
        gram = _mm_t(left, right)
        a_kb = jnp.where(strict, gram[:c, :c], 0.0)
        a_kk = jnp.where(strict, gram[:c, c:], 0.0)
        a_rb = jnp.where(incl, gram[c:, :c], 0.0)
        a_rk = jnp.where(incl, gram[c:, c:], 0.0)
        tinv = eye - a_kb
        npow = a_kb
        for _ in range(n_sq):
            nb = npow.astype(md)
            npow = _mm(nb, nb)
            tinv = tinv + _mm(tinv, npow.astype(md))
        vh = v[:, sl]
        vb = vh.astype(md)
        x = _mm_t(left[:c], s_mm) + _mm(a_kk, vb)
        w = _mm(tinv, x.astype(md))
        wb = w.astype(md)
        y = _mm_t(left[c:], s_mm) + _mm(a_rk, vb) - _mm(a_rb, wb)
        s_new = (s_old * d_end[:, sl] + _mm(vh.T, kte[:, sl].astype(md)) - _mm(w.T, bte[:, sl].astype(md)))
        s_scr[h] = s_new
        ym = jnp.mean(y, axis=-1, keepdims=True)
        yc = y - ym
        yn = yc * lax.rsqrt(jnp.mean(yc * yc, axis=-1, keepdims=True) + GN_EPS) * lng_ref[:, sl] + lnb_ref[:, sl]
        bonus = jnp.sum(bonus_rk[:, sl], axis=-1, keepdims=True) * vh
        o_ref[:, sl] = (yn + bonus) * g_ref[:, sl]

    @pl.when(ci == n_chunks - 1)
    def _():
        sout_ref[...] = s_scr[...]


def _rwkv_scan(pre, s0, p, c, mm_dtype):
    r = pre[0]
    b, l, _ = r.shape
    n_chunks = l // c
    rows = pl.BlockSpec((None, c, W_C), lambda bi, i: (bi, i, 0))
    state = pl.BlockSpec((None, H_C, N_C, N_C), lambda bi, i: (bi, 0, 0, 0))
    vec = _full((1, W_C))
    kern = functools.partial(_rwkv_scan_kernel, c=c, n_chunks=n_chunks, md=mm_dtype)
    return pl.pallas_call(
        kern, grid=(b, n_chunks),
        in_specs=[rows] * 7 + [state, vec, vec, vec],
        out_specs=[rows, state],
        out_shape=[jax.ShapeDtypeStruct((b, l, W_C), F32), jax.ShapeDtypeStruct((b, H_C, N_C, N_C), F32)],
        scratch_shapes=[pltpu.VMEM((H_C, N_C, N_C), F32)],
        compiler_params=_cparams("parallel", "arbitrary"), name="rwkv_scan")(
            *pre, s0, p['ln_g'], p['ln_b'], p['r_k'])


def _layernorm(y, g, b):
    mu = jnp.mean(y, axis=-1, keepdims=True)
    yc = y - mu
    return yc * lax.rsqrt(jnp.mean(yc * yc, axis=-1, keepdims=True) + LN_EPS) * g + b


def _row_valid(i, tm, tiles_per_batch, lo, hi):
    rowb = (i % tiles_per_batch) * tm + lax.broadcasted_iota(I32, (tm, 1), 0)
    return (rowb >= lo) & (rowb < hi)


def _outproj_kernel(oa_ref, ob_ref, oc_ref, x_ref, wa_ref, wb_ref, wc_ref, g_ref, b_ref, y_ref,
                    *, tm, tiles_per_batch, lo, hi, masked):
    mix = _mm(oa_ref[...], wa_ref[...]) + _mm(ob_ref[...], wb_ref[...]) + _mm(oc_ref[...], wc_ref[...])
    y = _layernorm(ALPHA * x_ref[...] + mix, g_ref[...], b_ref[...])
    if masked:
        y = jnp.where(_row_valid(pl.program_id(0), tm, tiles_per_batch, lo, hi), y, 0.0)
    y_ref[...] = y


def _outproj(oa, ob, oc, x2d, p, tm, rows_per_batch, lo, hi, masked):
    t = x2d.shape[0]
    row = lambda w: pl.BlockSpec((tm, w), lambda i: (i, 0))
    kern = functools.partial(_outproj_kernel, tm=tm, tiles_per_batch=rows_per_batch // tm, lo=lo, hi=hi,
                             masked=masked)
    return pl.pallas_call(
        kern, grid=(t // tm,),
        in_specs=[row(oa.shape[1]), row(ob.shape[1]), row(oc.shape[1]), row(D_MODEL),
                  _full(p['wo_a'].shape), _full(p['wo_b'].shape), _full(p['wo_c'].shape),
                  _full((1, D_MODEL)), _full((1, D_MODEL))],
        out_specs=row(D_MODEL), out_shape=jax.ShapeDtypeStruct((t, D_MODEL), F32),
        compiler_params=_cparams("parallel"), name="outproj_ln")(
            oa, ob, oc, x2d, p['wo_a'], p['wo_b'], p['wo_c'], p['ln1_g'], p['ln1_b'])


def _router_kernel(x_ref, whi_ref, wlo_ref, br_ref, dg_ref):
    x = x_ref[...]
    xh = x.astype(BF16)
    xl = (x - xh.astype(F32)).astype(BF16)
    logits = _mm(xh, whi_ref[...]) + _mm(xl, whi_ref[...]) + _mm(xh, wlo_ref[...]) + br_ref[...]
    lane = lax.broadcasted_iota(I32, (1, LANE), 1).astype(F32)
    big = float(LANE)
    gm = (lane >= N_EXPERTS) & (lane < N_EXPERTS + N_GROUPS)
    gl = jnp.where(gm, logits, -jnp.inf)
    ge = jnp.exp(gl - jnp.max(gl, axis=1, keepdims=True))
    gp = ge / jnp.sum(ge, axis=1, keepdims=True)
    gval = jnp.max(gp, axis=1, keepdims=True)
    gidx = jnp.min(jnp.where(gm & (gp == gval), lane, big), axis=1, keepdims=True) - N_EXPERTS
    elo = gidx * EXP_PER_GROUP
    em = (lane >= elo) & (lane < elo + EXP_PER_GROUP)
    el = jnp.where(em, logits, -jnp.inf)
    ee = jnp.exp(el - jnp.max(el, axis=1, keepdims=True))
    ep = jnp.where(em, ee / jnp.sum(ee, axis=1, keepdims=True), -1.0)
    p1 = jnp.max(ep, axis=1, keepdims=True)
    i1 = jnp.min(jnp.where(ep == p1, lane, big), axis=1, keepdims=True)
    ep2 = jnp.where(lane == i1, -1.0, ep)
    p2 = jnp.max(ep2, axis=1, keepdims=True)
    i2 = jnp.min(jnp.where((ep2 == p2) & (lane != i1), lane, big), axis=1, keepdims=True)
    den = p1 + p2
    dg_ref[...] = jnp.where(lane == i1, gval * p1 / den, 0.0) + jnp.where(lane == i2, gval * p2 / den, 0.0)


def _router(x2d, p, tm):
    t = x2d.shape[0]
    row = lambda w: pl.BlockSpec((tm, w), lambda i: (i, 0))
    return pl.pallas_call(
        _router_kernel, grid=(t // tm,),
        in_specs=[row(D_MODEL), _full((D_MODEL, LANE)), _full((D_MODEL, LANE)), _full((1, LANE))],
        out_specs=row(LANE), out_shape=jax.ShapeDtypeStruct((t, LANE), F32),
        compiler_params=_cparams("parallel"), name="router")(x2d, p['wr_hi'], p['wr_lo'], p['br'])


def _moe_kernel(x_ref, dg_ref, wgu_ref, wd_ref, g_ref, b_ref, y_ref, acc_ref, xb_ref,
                *, tm, tiles_per_batch, lo, hi, masked):
    e = pl.program_id(1)

    @pl.when(e == 0)
    def _():
        xb_ref[...] = x_ref[...].astype(xb_ref.dtype)
        acc_ref[...] = jnp.zeros_like(acc_ref)

    hcat = _mm(xb_ref[...], wgu_ref[...])
    hg, hu = hcat[:, :D_EXPERT], hcat[:, D_EXPERT:]
    lane = lax.broadcasted_iota(I32, (1, LANE), 1)
    gate = jnp.sum(jnp.where(lane == e, dg_ref[...], 0.0), axis=1, keepdims=True)
    acc_ref[...] += _mm(hg * jax.nn.sigmoid(hg) * hu * gate, wd_ref[...])

    @pl.when(e == N_EXPERTS - 1)
    def _():
        y = _layernorm(ALPHA * x_ref[...] + acc_ref[...], g_ref[...], b_ref[...])
        if masked:
            y = jnp.where(_row_valid(pl.program_id(0), tm, tiles_per_batch, lo, hi), y, 0.0)
        y_ref[...] = y


def _moe(x2d, dg, p, tm, rows_per_batch, lo, hi, masked):
    t = x2d.shape[0]
    row = lambda w: pl.BlockSpec((tm, w), lambda i, e: (i, 0))
    kern = functools.partial(_moe_kernel, tm=tm, tiles_per_batch=rows_per_batch // tm, lo=lo, hi=hi,
                             masked=masked)
    return pl.pallas_call(
        kern, grid=(t // tm, N_EXPERTS),
        in_specs=[row(D_MODEL), row(LANE),
                  pl.BlockSpec((None, D_MODEL, 2 * D_EXPERT), lambda i, e: (e, 0, 0)),
                  pl.BlockSpec((None, D_EXPERT, D_MODEL), lambda i, e: (e, 0, 0)),
                  pl.BlockSpec((1, D_MODEL), lambda i, e: (0, 0)), pl.BlockSpec((1, D_MODEL), lambda i, e: (0, 0))],
        out_specs=row(D_MODEL), out_shape=jax.ShapeDtypeStruct((t, D_MODEL), F32),
        scratch_shapes=[pltpu.VMEM((tm, D_MODEL), F32), pltpu.VMEM((tm, D_MODEL), p['wgu'].dtype)],
        compiler_params=_cparams("parallel", "arbitrary"), name="moe_ln")(
            x2d, dg, p['wgu'], p['wd'], p['ln2_g'], p['ln2_b'])


def _rope_table(pos, pattern):
    posf = pos.astype(F32)[:, None]
    n = pos.shape[0]
    cos, s_lo, s_hi = [], [], []
    for kind, w in pattern:
        if kind == 'rope':
            half = w // 2
            inv = ROPE_THETA ** (-jnp.arange(half, dtype=F32) / half)
            ang = posf * inv[None, :]
            c, s, z = jnp.cos(ang), jnp.sin(ang), jnp.zeros((n, half), F32)
            cos += [c, c]
            s_lo += [-s, z]
            s_hi += [z, s]
        else:
            fill = jnp.full((n, w), 1.0 if kind == 'one' else 0.0, F32)
            z = jnp.zeros((n, w), F32)
            cos.append(fill)
            s_lo.append(z)
            s_hi.append(z)
    return jnp.stack([jnp.concatenate(t, axis=1) for t in (cos, s_lo, s_hi)])


def _tables(pos):
    return {
        'mla_q': _rope_table(pos, [('one', NOPE_A), ('rope', ROPE_A), ('zero', HEAD_W - NOPE_A - ROPE_A)]),
        'mla_kr': _rope_table(pos, [('rope', ROPE_A), ('zero', LANE - ROPE_A)]),
        'dsa_qk': _rope_table(pos, [('rope', DH_B), ('zero', HEAD_W - DH_B)]),
        'idx_q': _rope_table(pos, [('rope', D_I)] * (LANE // D_I)),
        'idx_k': _rope_table(pos, [('rope', D_I), ('one', H_I), ('zero', LANE - D_I - H_I)]),
    }


def _pad_cols(w, width):
    return jnp.pad(w, [(0, 0)] * (w.ndim - 1) + [(0, width - w.shape[-1])])


def _head_pad(w, n_heads, width):
    d = w.shape[-1] // n_heads
    w = w.reshape(w.shape[:-1] + (n_heads, d))
    return _pad_cols(w, width).reshape(w.shape[:-2] + (n_heads * width,))


def _prepare_weights(w, md):
    w_in = w['w_in']
    a, bseg, cseg = w_in[..., :A_COLS], w_in[..., A_COLS:A_COLS + B_COLS], w_in[..., A_COLS + B_COLS:]
    hd = H_B * DH_B
    wb = jnp.concatenate([
        _head_pad(bseg[..., 0:hd], H_B, HEAD_W), _head_pad(bseg[..., hd:2 * hd], H_B, HEAD_W),
        _head_pad(bseg[..., 2 * hd:3 * hd], H_B, HEAD_W), bseg[..., 3 * hd:3 * hd + H_I * D_I],
        _pad_cols(bseg[..., 3 * hd + H_I * D_I:], LANE)], axis=-1)
    w_uq = w['mla_w_uq'].reshape(DEPTH, Q_LORA, H_A, NOPE_A + ROPE_A)
    w_ukv = w['mla_w_ukv'].reshape(DEPTH, KV_LORA, H_A, NOPE_A + V_A)
    zeros_lora = lambda n: jnp.zeros((DEPTH, n, W_C), F32)
    head_of = jnp.arange(W_C) // N_C
    wr = jnp.concatenate([w['router_expert'], w['router_group']], axis=-1)
    wr = _pad_cols(wr, LANE)
    wr_hi = wr.astype(BF16)
    row = lambda v: v.reshape(DEPTH, 1, -1)
    return {
        'wa': _pad_cols(a, A_W).astype(md), 'wb': wb.astype(md), 'wc': cseg.astype(md),
        'qg': row(w['mla_q_norm']), 'kg': row(w['mla_kv_norm']),
        'wuq': _pad_cols(w_uq, HEAD_W).reshape(DEPTH, Q_LORA, H_A * HEAD_W).astype(md),
        'wk': _pad_cols(w_ukv[..., :NOPE_A], HEAD_W).reshape(DEPTH, KV_LORA, H_A * HEAD_W).astype(md),
        'wv': _pad_cols(w_ukv[..., NOPE_A:], HEAD_W).reshape(DEPTH, KV_LORA, H_A * HEAD_W).astype(md),
        'mu': row(w['rwkv_mu']), 'w0': row(w['rwkv_w0']), 'a0': row(w['rwkv_a0']),
        'k_k': row(w['rwkv_k_k']), 'k_a': row(w['rwkv_k_a']), 'r_k': row(w['rwkv_r_k']),
        'ln_g': row(w['rwkv_ln_g']), 'ln_b': row(w['rwkv_ln_b']),
        'wup': jnp.concatenate([w['rwkv_w_up'], zeros_lora(LANE - W_LORA)], axis=1).astype(md),
        'aup': jnp.concatenate([zeros_lora(W_LORA), w['rwkv_a_up'], zeros_lora(G_LORA)], axis=1).astype(md),
        'gup': jnp.concatenate([zeros_lora(W_LORA + A_LORA), w['rwkv_g_up']], axis=1).astype(md),
        'bd': jnp.broadcast_to((head_of[:, None] == head_of[None, :]).astype(md), (DEPTH, W_C, W_C)),
        'wo_a': w['w_out'][:, :H_A * V_A].astype(md),
        'wo_b': w['w_out'][:, H_A * V_A:H_A * V_A + hd].astype(md),
        'wo_c': w['w_out'][:, H_A * V_A + hd:].astype(md),
        'ln1_g': row(w['ln1_g']), 'ln1_b': row(w['ln1_b']), 'ln2_g': row(w['ln2_g']), 'ln2_b': row(w['ln2_b']),
        'wr_hi': wr_hi, 'wr_lo': (wr - wr_hi.astype(F32)).astype(BF16),
        'br': _pad_cols(jnp.concatenate([w['router_expert_b'], w['router_group_b']], axis=-1), LANE).reshape(DEPTH, 1, LANE),
        'wgu': jnp.concatenate([w['exp_w_gate'], w['exp_w_up']], axis=-1).astype(md),
        'wd': w['exp_w_down'].astype(md),
    }


def _largest_tile(n, limit):
    best = 8
    for t in range(8, min(n, limit) + 1, 8):
        if n % t == 0:
            best = t
    return best


def _trunk(x, pos, past, wp, *, lo, hi, causal, k_sel):
    b, l, _ = x.shape
    t = b * l
    tabs = _tables(pos)
    tl = min(256, l)
    tq_mla = min(128, l)
    tq_dsa = min(CHUNK, l)
    c_rwkv = min(CHUNK, l)
    masked = not (lo == 0 and hi == l)
    md = wp['wa'].dtype
    tm_cap = 512 if md == BF16 else 128
    tm = _largest_tile(l, tm_cap) if masked else _largest_tile(t, tm_cap)
    tm_moe = _largest_tile(l, 1100) if masked else _largest_tile(t, 1100)
    first_key = lo if causal else 0
    new = {k: [] for k in ('ckv', 'krope', 'dsa_k', 'dsa_v', 'dsa_kidx', 'rwkv', 'shift')}
    x2d = x.reshape(t, D_MODEL)
    for layer in range(DEPTH):
        p = {k: v[layer] for k, v in wp.items()}
        za, zb, zc = _proj_in(x2d, p['wa'], p['wb'], p['wc'], tm)
        za3, zb3, zc3 = za.reshape(b, l, A_W), zb.reshape(b, l, B_W), zc.reshape(b, l, C_W)

        q, kc, v, ckv_new, kr_new = _mla_prep(za3, tabs['mla_q'], tabs['mla_kr'], p['qg'], p['kg'],
                                              p['wuq'], p['wk'], p['wv'], tl)
        mla_past = None
        if past is not None:
            mla_past = _mla_past(past['ckv'][layer], _pad_cols(past['krope'][layer], LANE), p['wk'], p['wv'], 512)
        oa = _mla_attn(q, kc, v, mla_past, tq_mla, causal, first_key)

        qd, kd, vd, qi, ki, wi, kst, vst, kist = _dsa_prep(zb3, tabs['dsa_qk'], tabs['idx_q'], tabs['idx_k'], tl, md)
        dsa_past = None
        if past is not None:
            to_heads = lambda c: _pad_cols(jnp.swapaxes(c.astype(md), 1, 2), HEAD_W)
            dsa_past = (to_heads(past['dsa_k'][layer]), to_heads(past['dsa_v'][layer]),
                        past['dsa_kidx'][layer].astype(md))
        ob = _dsa_attn(qd, qi, wi, kd, vd, ki, dsa_past, tq_dsa, causal, first_key, k_sel)

        if past is not None:
            shift, s0 = past['shift'][layer][:, None, :], past['rwkv'][layer]
        else:
            shift, s0 = jnp.zeros((b, 1, C_W), F32), jnp.zeros((b, H_C, N_C, N_C), F32)
        pre = _rwkv_prep(zc3, shift, p, tl, lo, hi)
        oc, s_last = _rwkv_scan(pre, s0, p, c_rwkv, md)

        x1 = _outproj(oa.reshape(t, -1), ob.reshape(t, -1), oc.reshape(t, -1), x2d, p, tm, l, lo, hi, masked)
        dg = _router(x1, p, tm)
        x2d = _moe(x1, dg, p, tm_moe, l, lo, hi, masked)

        n = hi - lo
        new['ckv'].append(ckv_new[:, lo:hi])
        new['krope'].append(kr_new[:, lo:hi])
        new['dsa_k'].append(kst[:, lo:hi].reshape(b, n, H_B, DH_B))
        new['dsa_v'].append(vst[:, lo:hi].reshape(b, n, H_B, DH_B))
        new['dsa_kidx'].append(kist[:, lo:hi])
        new['rwkv'].append(s_last)
        new['shift'].append(zc3[:, hi - 1, :])
    return x2d.reshape(b, l, D_MODEL), {k: jnp.stack(v) for k, v in new.items()}


def kernel(x_prompt, x_sample, cache_mla_ckv, cache_mla_krope, cache_dsa_k, cache_dsa_v, cache_dsa_kidx, state_rwkv, state_rwkv_shift, meta_tokens, w_in, mla_q_norm, mla_kv_norm, mla_w_uq, mla_w_ukv, rwkv_mu, rwkv_w0, rwkv_w_up, rwkv_a0, rwkv_a_up, rwkv_g_up, rwkv_k_k, rwkv_k_a, rwkv_r_k, rwkv_ln_g, rwkv_ln_b, w_out, ln1_g, ln1_b, ln2_g, ln2_b, router_group, router_group_b, router_expert, router_expert_b, exp_w_gate, exp_w_up, exp_w_down):
    weights = {
        'w_in': w_in, 'mla_q_norm': mla_q_norm, 'mla_kv_norm': mla_kv_norm, 'mla_w_uq': mla_w_uq,
        'mla_w_ukv': mla_w_ukv, 'rwkv_mu': rwkv_mu, 'rwkv_w0': rwkv_w0, 'rwkv_w_up': rwkv_w_up,
        'rwkv_a0': rwkv_a0, 'rwkv_a_up': rwkv_a_up, 'rwkv_g_up': rwkv_g_up, 'rwkv_k_k': rwkv_k_k,
        'rwkv_k_a': rwkv_k_a, 'rwkv_r_k': rwkv_r_k, 'rwkv_ln_g': rwkv_ln_g, 'rwkv_ln_b': rwkv_ln_b,
        'w_out': w_out, 'ln1_g': ln1_g, 'ln1_b': ln1_b, 'ln2_g': ln2_g, 'ln2_b': ln2_b,
        'router_group': router_group, 'router_group_b': router_group_b, 'router_expert': router_expert,
        'router_expert_b': router_expert_b, 'exp_w_gate': exp_w_gate, 'exp_w_up': exp_w_up,
        'exp_w_down': exp_w_down,
    }
    wp = _prepare_weights(weights, BF16)
    wp_sample = _prepare_weights(weights, F32)

    bp, seq, _ = x_prompt.shape
    lo, hi = FRONT_PAD, FRONT_PAD + N_META + seq
    lp = -(-hi // KEY_TILE) * KEY_TILE
    meta = jnp.broadcast_to(meta_tokens.astype(F32)[None], (bp, N_META, D_MODEL))
    xp = jnp.concatenate([jnp.zeros((bp, lo, D_MODEL), F32), meta, x_prompt,
                          jnp.zeros((bp, lp - hi, D_MODEL), F32)], axis=1)
    pos_p = jnp.arange(lp, dtype=jnp.int32) - lo
    yp, new_p = _trunk(xp, pos_p, None, wp, lo=lo, hi=hi, causal=True, k_sel=min(TOPK_MAX, seq // 4))
    y_prompt = yp[:, lo + N_META:hi]

    n_past, n_new = cache_mla_ckv.shape[2], x_sample.shape[1]
    assert n_past % KEY_TILE == 0 and n_new <= CHUNK and n_new % 8 == 0, (n_past, n_new)
    pos_s = n_past + jnp.arange(n_new, dtype=jnp.int32)
    past_s = {'ckv': cache_mla_ckv, 'krope': cache_mla_krope, 'dsa_k': cache_dsa_k, 'dsa_v': cache_dsa_v,
              'dsa_kidx': cache_dsa_kidx, 'rwkv': state_rwkv, 'shift': state_rwkv_shift}
    y_sample, new_s = _trunk(x_sample, pos_s, past_s, wp_sample, lo=0, hi=n_new, causal=False,
                             k_sel=min(TOPK_MAX, (n_past + n_new) // 4))

    keys = ('ckv', 'krope', 'dsa_k', 'dsa_v', 'dsa_kidx', 'rwkv', 'shift')
    return (y_prompt, y_sample) + tuple(new_p[k] for k in keys) + tuple(new_s[k] for k in keys)
```

```python
import functools

import jax
import jax.numpy as jnp
from jax import lax
from jax.experimental import pallas as pl
from jax.experimental.pallas import tpu as pltpu

F32 = jnp.float32
BF16 = jnp.bfloat16
I32 = jnp.int32

D_MODEL = 1024
DEPTH = 4
CHUNK = 64
N_META = 16
ROPE_THETA = 10000.0
NEG_INF = -1e30
LN_EPS = 1e-5
RMS_EPS = 1e-6
GN_EPS = 64e-5
ALPHA = (2 * DEPTH) ** 0.25

H_A, Q_LORA, KV_LORA, NOPE_A, ROPE_A, V_A = 4, 256, 128, 64, 32, 64
H_B, DH_B, H_I, D_I, TOPK_MAX = 4, 64, 8, 32, 256
H_C, N_C, W_C, W_LORA, A_LORA, G_LORA = 8, 64, 512, 32, 32, 64
N_GROUPS, EXP_PER_GROUP, N_EXPERTS, D_EXPERT = 4, 4, 16, 256

A_COLS = Q_LORA + KV_LORA + ROPE_A
B_COLS = 3 * H_B * DH_B + H_I * D_I + D_I + H_I
C_COLS = 3 * W_C + W_LORA + A_LORA + G_LORA

MLA_SCALE = (NOPE_A + ROPE_A) ** -0.5
DSA_SCALE = DH_B ** -0.5
IDX_SCALE = (H_I ** -0.5) * (D_I ** -0.5)

LANE = 128
HEAD_W = 128
A_W = Q_LORA + KV_LORA + LANE
B_Q, B_K, B_V = 0, H_B * HEAD_W, 2 * H_B * HEAD_W
B_QI = 3 * H_B * HEAD_W
B_KI = B_QI + H_I * D_I
B_W = B_KI + LANE
C_W = C_COLS
FRONT_PAD = CHUNK - N_META
KEY_TILE = 256
INT_MIN = -2147483648
INT_MAX = 2147483647
VMEM_LIMIT = 56 * 1024 * 1024


def _cparams(*sem):
    return pltpu.CompilerParams(dimension_semantics=sem, vmem_limit_bytes=VMEM_LIMIT)


def _full(shape):
    nd = len(shape)
    return pl.BlockSpec(shape, lambda *_: (0,) * nd)


def _tile_lanes(t, reps):
    return t if reps == 1 else jnp.concatenate([t] * reps, axis=1)


def _rope(x, tab_ref, half):
    n = x.shape[1]
    reps = n // LANE
    cos = _tile_lanes(tab_ref[0], reps)
    sin_lo = _tile_lanes(tab_ref[1], reps)
    sin_hi = _tile_lanes(tab_ref[2], reps)
    return x * cos + pltpu.roll(x, n - half, 1) * sin_lo + pltpu.roll(x, half, 1) * sin_hi


def _mm(a, b):
    if b.dtype == F32:
        return jnp.dot(a.astype(F32), b, preferred_element_type=F32, precision=lax.Precision.HIGHEST)
    return jnp.dot(a.astype(BF16), b, preferred_element_type=F32)


def _mm_t(a, b):
    prec = lax.Precision.HIGHEST if b.dtype == F32 else None
    return lax.dot_general(a.astype(b.dtype), b, (((1,), (1,)), ((), ())), preferred_element_type=F32,
                           precision=prec)


def _pair_heads(parts):
    lo = parts[0] + pltpu.roll(parts[1], 64, 1)
    hi = parts[2] + pltpu.roll(parts[3], 64, 1)
    return jnp.concatenate([lo, hi], axis=1)


def _proj_in_kernel(x_ref, wa_ref, wb_ref, wc_ref, za_ref, zb_ref, zc_ref):
    xb = x_ref[...].astype(wa_ref.dtype)
    za_ref[...] = _mm(xb, wa_ref[...])
    zb_ref[...] = _mm(xb, wb_ref[...])
    zc_ref[...] = _mm(xb, wc_ref[...])


def _proj_in(x2d, wa, wb, wc, tm):
    t = x2d.shape[0]
    row = lambda w: pl.BlockSpec((tm, w), lambda i: (i, 0))
    return pl.pallas_call(
        _proj_in_kernel, grid=(t // tm,),
        in_specs=[row(D_MODEL), _full(wa.shape), _full(wb.shape), _full(wc.shape)],
        out_specs=[row(A_W), row(B_W), row(C_W)],
        out_shape=[jax.ShapeDtypeStruct((t, w), F32) for w in (A_W, B_W, C_W)],
        compiler_params=_cparams("parallel"), name="proj_in")(x2d, wa, wb, wc)


def _rms(x, g):
    return x * lax.rsqrt(jnp.mean(x * x, axis=-1, keepdims=True) + RMS_EPS) * g


def _mla_prep_kernel(za_ref, tq_ref, tk_ref, qg_ref, kg_ref, wuq_ref, wk_ref, wv_ref,
                     q_ref, kc_ref, v_ref, ckv_ref, kr_ref):
    za = za_ref[...]
    qn = _rms(za[:, :Q_LORA], qg_ref[...])
    q = _rope(_mm(qn, wuq_ref[...]), tq_ref, ROPE_A // 2) * MLA_SCALE
    ckvn = _rms(za[:, Q_LORA:Q_LORA + KV_LORA], kg_ref[...])
    ckv_ref[...] = ckvn
    kr = _rope(za[:, Q_LORA + KV_LORA:], tk_ref, ROPE_A // 2)
    kr_ref[...] = kr[:, :ROPE_A]
    kc = _mm(ckvn, wk_ref[...]) + _tile_lanes(pltpu.roll(kr, NOPE_A, 1), H_A)
    v = _mm(ckvn, wv_ref[...]) + _ones_lane(H_A * HEAD_W)
    for h in range(H_A):
        sl = slice(h * HEAD_W, (h + 1) * HEAD_W)
        q_ref[h] = q[:, sl].astype(q_ref.dtype)
        kc_ref[h] = kc[:, sl].astype(kc_ref.dtype)
        v_ref[h] = v[:, sl].astype(v_ref.dtype)


def _mla_prep(za3, tq, tk, qg, kg, wuq, wk, wv, tl):
    b, l, _ = za3.shape
    heads = pl.BlockSpec((None, H_A, tl, HEAD_W), lambda bi, i: (bi, 0, i, 0))
    tab = pl.BlockSpec((3, tl, LANE), lambda bi, i: (0, i, 0))
    hshape = jax.ShapeDtypeStruct((b, H_A, l, HEAD_W), wuq.dtype)
    return pl.pallas_call(
        _mla_prep_kernel, grid=(b, l // tl),
        in_specs=[pl.BlockSpec((None, tl, A_W), lambda bi, i: (bi, i, 0)), tab, tab,
                  _full(qg.shape), _full(kg.shape), _full(wuq.shape), _full(wk.shape), _full(wv.shape)],
        out_specs=[heads, heads, heads,
                   pl.BlockSpec((None, tl, KV_LORA), lambda bi, i: (bi, i, 0)),
                   pl.BlockSpec((None, tl, ROPE_A), lambda bi, i: (bi, i, 0))],
        out_shape=[hshape, hshape, hshape,
                   jax.ShapeDtypeStruct((b, l, KV_LORA), F32), jax.ShapeDtypeStruct((b, l, ROPE_A), F32)],
        compiler_params=_cparams("parallel", "parallel"), name="mla_prep")(za3, tq, tk, qg, kg, wuq, wk, wv)


def _mla_past_kernel(ckv_ref, kr_ref, wk_ref, wv_ref, kc_ref, v_ref):
    cb = ckv_ref[...]
    kc = _mm(cb, wk_ref[...]) + _tile_lanes(pltpu.roll(kr_ref[...], NOPE_A, 1), H_A)
    v = _mm(cb, wv_ref[...]) + _ones_lane(H_A * HEAD_W)
    for h in range(H_A):
        sl = slice(h * HEAD_W, (h + 1) * HEAD_W)
        kc_ref[h] = kc[:, sl].astype(kc_ref.dtype)
        v_ref[h] = v[:, sl].astype(v_ref.dtype)


def _mla_past(ckv, kr128, wk, wv, tl):
    b, p, _ = ckv.shape
    heads = pl.BlockSpec((None, H_A, tl, HEAD_W), lambda bi, i: (bi, 0, i, 0))
    hshape = jax.ShapeDtypeStruct((b, H_A, p, HEAD_W), wk.dtype)
    return pl.pallas_call(
        _mla_past_kernel, grid=(b, p // tl),
        in_specs=[pl.BlockSpec((None, tl, KV_LORA), lambda bi, i: (bi, i, 0)),
                  pl.BlockSpec((None, tl, LANE), lambda bi, i: (bi, i, 0)), _full(wk.shape), _full(wv.shape)],
        out_specs=[heads, heads], out_shape=[hshape, hshape],
        compiler_params=_cparams("parallel", "parallel"), name="mla_past")(ckv, kr128, wk, wv)


def _ones_lane(width):
    lane = lax.broadcasted_iota(I32, (1, width), 1)
    return jnp.where((lane & (HEAD_W - 1)) == V_A, 1.0, 0.0)


def _fold_lanes_max(s):
    out = s[:, :LANE]
    for c0 in range(LANE, s.shape[1], LANE):
        out = jnp.maximum(out, s[:, c0:c0 + LANE])
    return out


def _attend(q_ref, sources, s_scr, tq, n_heads, mask_fn):
    heads = range(n_heads)
    qs = [q_ref[h] for h in heads]
    mrun = tuple(jnp.full((tq, LANE), NEG_INF, F32) for _ in heads)
    for (k_ref, _, base, tk, nt, masked) in sources:
        def score_body(j, ms, k_ref=k_ref, base=base, tk=tk, masked=masked):
            off = pl.multiple_of(j * tk, tk)
            col = pl.multiple_of(base + off, tk)
            ss = [_mm_t(qs[h], k_ref[h, pl.ds(off, tk), :]) for h in heads]
            if masked:
                mask = mask_fn(off, col, tk)
                ss = [jnp.where(mask, s, NEG_INF) for s in ss]
            for h in heads:
                s_scr[h, :, pl.ds(col, tk)] = ss[h]
            if tk % LANE == 0:
                return tuple(jnp.maximum(ms[h], _fold_lanes_max(ss[h])) for h in heads)
            return tuple(jnp.maximum(ms[h], jnp.max(ss[h], axis=1, keepdims=True)) for h in heads)
        mrun = lax.fori_loop(0, nt, score_body, mrun)
    m = [jnp.max(mr, axis=1, keepdims=True) for mr in mrun]
    accs = tuple(jnp.zeros((tq, HEAD_W), F32) for _ in heads)
    for (_, v_ref, base, tk, nt, _) in sources:
        def pv_body(j, acc, v_ref=v_ref, base=base, tk=tk):
            off = pl.multiple_of(j * tk, tk)
            col = pl.multiple_of(base + off, tk)
            ps = [jnp.exp(s_scr[h, :, pl.ds(col, tk)] - m[h]) for h in heads]
            return tuple(acc[h] + _mm(ps[h], v_ref[h, pl.ds(off, tk), :]) for h in heads)
        accs = lax.fori_loop(0, nt, pv_body, accs)
    lane = lax.broadcasted_iota(I32, (1, HEAD_W), 1)
    return [jnp.where(lane < V_A, a / a[:, V_A:V_A + 1], 0.0) for a in accs]


def _mla_attn_kernel(*refs, tq, tkn, ln, causal, first_key, p_len, tkp):
    if p_len:
        q_ref, kp_ref, vp_ref, kn_ref, vn_ref, o_ref, s_scr = refs
    else:
        q_ref, kn_ref, vn_ref, o_ref, s_scr = refs
    qt = pl.program_id(1)
    if causal:
        row = qt * tq + lax.broadcasted_iota(I32, (tq, 1), 0)
        nvis = ((row >> 6) + 1) << 6
        n_tiles = (qt * tq + tq + tkn - 1) // tkn
    else:
        nvis = None
        n_tiles = ln // tkn
    sources = []
    if p_len:
        sources.append((kp_ref, vp_ref, 0, tkp, p_len // tkp, False))
    sources.append((kn_ref, vn_ref, p_len, tkn, n_tiles, causal))

    def mask_fn(off, col, tk):
        key = off + lax.broadcasted_iota(I32, (1, tk), 1)
        return (key >= first_key) & (key < nvis)

    o_ref[...] = _pair_heads(_attend(q_ref, sources, s_scr, tq, H_A, mask_fn))


def _mla_attn(q, kn, vn, past, tq, causal, first_key):
    b, _, l, _ = q.shape
    ln = kn.shape[2]
    tkn = KEY_TILE if ln % KEY_TILE == 0 else ln
    qspec = pl.BlockSpec((None, H_A, tq, HEAD_W), lambda bi, i: (bi, 0, i, 0))
    whole = lambda n: pl.BlockSpec((None, H_A, n, HEAD_W), lambda bi, i: (bi, 0, 0, 0))
    args, specs, p_len = [q], [qspec], 0
    if past is not None:
        p_len = past[0].shape[2]
        args += list(past)
        specs += [whole(p_len), whole(p_len)]
    args += [kn, vn]
    specs += [whole(ln), whole(ln)]
    kern = functools.partial(_mla_attn_kernel, tq=tq, tkn=tkn, ln=ln, causal=causal, first_key=first_key,
                             p_len=p_len, tkp=KEY_TILE)
    return pl.pallas_call(
        kern, grid=(b, l // tq), in_specs=specs,
        out_specs=pl.BlockSpec((None, tq, H_A * V_A), lambda bi, i: (bi, i, 0)),
        out_shape=jax.ShapeDtypeStruct((b, l, H_A * V_A), F32),
        scratch_shapes=[pltpu.VMEM((H_A, tq, p_len + ln), F32)],
        compiler_params=_cparams("parallel", "arbitrary"), name="mla_attn")(*args)


def _dsa_prep_kernel(zb_ref, tqk_ref, tiq_ref, tik_ref,
                     q_ref, k_ref, v_ref, qi_ref, ki_ref, wi_ref, kst_ref, vst_ref, kist_ref):
    zb = zb_ref[...]
    q = _rope(zb[:, B_Q:B_K], tqk_ref, DH_B // 2) * DSA_SCALE
    k = _rope(zb[:, B_K:B_V], tqk_ref, DH_B // 2)
    v = zb[:, B_V:B_QI]
    qi = _rope(zb[:, B_QI:B_KI], tiq_ref, D_I // 2)
    kiw = _rope(zb[:, B_KI:B_W], tik_ref, D_I // 2)
    ks, vs = [], []
    for h in range(H_B):
        sl = slice(h * HEAD_W, (h + 1) * HEAD_W)
        q_ref[h] = q[:, sl].astype(q_ref.dtype)
        k_ref[h] = k[:, sl].astype(k_ref.dtype)
        v_ref[h] = (v[:, sl] + _ones_lane(HEAD_W)).astype(v_ref.dtype)
        ks.append(k[:, sl])
        vs.append(v[:, sl])
    for h in range(H_I):
        qi_ref[h] = qi[:, h * D_I:(h + 1) * D_I].astype(qi_ref.dtype)
    ki_ref[...] = kiw[:, :D_I].astype(ki_ref.dtype)
    wi_ref[...] = kiw * IDX_SCALE
    kst_ref[...] = _pair_heads(ks)
    vst_ref[...] = _pair_heads(vs)
    kist_ref[...] = kiw[:, :D_I]


def _dsa_prep(zb3, tqk, tiq, tik, tl, mm_dtype):
    b, l, _ = zb3.shape
    heads = pl.BlockSpec((None, H_B, tl, HEAD_W), lambda bi, i: (bi, 0, i, 0))
    tab = pl.BlockSpec((3, tl, LANE), lambda bi, i: (0, i, 0))
    rows = lambda w: pl.BlockSpec((None, tl, w), lambda bi, i: (bi, i, 0))
    hshape = jax.ShapeDtypeStruct((b, H_B, l, HEAD_W), mm_dtype)
    return pl.pallas_call(
        _dsa_prep_kernel, grid=(b, l // tl),
        in_specs=[rows(B_W), tab, tab, tab],
        out_specs=[heads, heads, heads,
                   pl.BlockSpec((None, H_I, tl, D_I), lambda bi, i: (bi, 0, i, 0)),
                   rows(D_I), rows(LANE), rows(H_B * DH_B), rows(H_B * DH_B), rows(D_I)],
        out_shape=[hshape, hshape, hshape,
                   jax.ShapeDtypeStruct((b, H_I, l, D_I), mm_dtype),
                   jax.ShapeDtypeStruct((b, l, D_I), mm_dtype),
                   jax.ShapeDtypeStruct((b, l, LANE), F32),
                   jax.ShapeDtypeStruct((b, l, H_B * DH_B), F32),
                   jax.ShapeDtypeStruct((b, l, H_B * DH_B), F32),
                   jax.ShapeDtypeStruct((b, l, D_I), F32)],
        compiler_params=_cparams("parallel", "parallel"), name="dsa_prep")(zb3, tqk, tiq, tik)


def _order_key(x):
    x = jnp.where(x == 0.0, 0.0, x)
    b = pltpu.bitcast(x, I32)
    return jnp.where(b < 0, b ^ INT_MAX, b)


def _dsa_attn_kernel(*refs, tq, tkn, ln, causal, first_key, p_len, tkp, k_sel):
    if p_len:
        (q_ref, qi_ref, wi_ref, kp_ref, vp_ref, kip_ref, kn_ref, vn_ref, kin_ref, o_ref, keys_ref, s_scr) = refs
    else:
        (q_ref, qi_ref, wi_ref, kn_ref, vn_ref, kin_ref, o_ref, keys_ref, s_scr) = refs
    qt = pl.program_id(1)
    p_tiles = p_len // tkp if p_len else 0
    if causal:
        row = qt * tq + lax.broadcasted_iota(I32, (tq, 1), 0)
        nvis = ((row >> 6) + 1) << 6
        n_tiles = (qt * tq + tq + tkn - 1) // tkn
        n_valid = qt * tq + tq - first_key + p_len
    else:
        nvis = ln
        n_tiles = ln // tkn
        n_valid = ln + p_len

    sources = []
    if p_len:
        sources.append((kp_ref, vp_ref, kip_ref, 0, tkp, p_tiles, False))
    sources.append((kn_ref, vn_ref, kin_ref, p_len, tkn, n_tiles, causal))

    def col_mask(off, tk):
        col = off + lax.broadcasted_iota(I32, (1, tk), 1)
        return (col >= first_key) & (col < nvis)

    wi = wi_ref[...]
    qis = [qi_ref[h] for h in range(H_I)]
    wis = [wi[:, D_I + h:D_I + h + 1] for h in range(H_I)]
    for (_, _, ki_ref, base, tk, nt, masked) in sources:
        def score_body(j, c, ki_ref=ki_ref, base=base, tk=tk, masked=masked):
            off = pl.multiple_of(j * tk, tk)
            ki_t = ki_ref[pl.ds(off, tk), :]
            sc = jnp.maximum(_mm_t(qis[0], ki_t), 0.0) * wis[0]
            for h in range(1, H_I):
                sc = sc + jnp.maximum(_mm_t(qis[h], ki_t), 0.0) * wis[h]
            key = _order_key(sc)
            if masked:
                key = jnp.where(col_mask(off, tk), key, INT_MIN)
            keys_ref[:, pl.ds(pl.multiple_of(base + off, tk), tk)] = key
            return c
        lax.fori_loop(0, nt, score_body, 0)

    count_sources = []
    for (_, _, _, base, tk, nt, _) in sources:
        if tk == KEY_TILE:
            count_sources.append((base, 2 * tk, (nt + 1) // 2))
        else:
            count_sources.append((base, tk, nt))
    if tkn == KEY_TILE:
        @pl.when(n_tiles % 2 == 1)
        def _():
            keys_ref[:, pl.ds(pl.multiple_of(p_len + n_tiles * tkn, tkn), tkn)] = jnp.full((tq, tkn), INT_MIN, I32)

    def count_ge(mid):
        total = jnp.zeros((tq, 1), F32)
        for (base, tk, nt) in count_sources:
            wacc = min(tk, LANE)

            def cnt_body(j, acc, base=base, tk=tk, wacc=wacc):
                off = pl.multiple_of(base + j * tk, tk)
                hit = jnp.where(keys_ref[:, pl.ds(off, tk)] >= mid, 1.0, 0.0)
                for c0 in range(0, tk, wacc):
                    acc = acc + hit[:, c0:c0 + wacc]
                return acc
            acc = lax.fori_loop(0, nt, cnt_body, jnp.zeros((tq, wacc), F32))
            total = total + jnp.sum(acc, axis=1, keepdims=True)
        return total

    def bisect_body(_, c):
        lo, hi = c
        mid = (lo >> 1) + (hi >> 1) + (((lo & 1) + (hi & 1) + 1) >> 1)
        ge = count_ge(mid) >= float(k_sel)
        return jnp.where(ge, mid, lo), jnp.where(ge, hi, mid - 1)

    n_iter = jnp.where(n_valid > k_sel, 32, 0)
    thr, _ = lax.fori_loop(0, n_iter, bisect_body,
                           (jnp.full((tq, 1), INT_MIN + 1, I32), jnp.full((tq, 1), INT_MAX, I32)))

    def selected(off, col, tk):
        return keys_ref[:, pl.ds(col, tk)] >= thr

    att_sources = [(k_ref, v_ref, base, tk, nt, True) for (k_ref, v_ref, _, base, tk, nt, _) in sources]
    o_ref[...] = _pair_heads(_attend(q_ref, att_sources, s_scr, tq, H_B, selected))


def _dsa_attn(q, qi, wi, kn, vn, kin, past, tq, causal, first_key, k_sel):
    b, _, l, _ = q.shape
    ln = kn.shape[2]
    tkn = KEY_TILE if ln % KEY_TILE == 0 else ln
    qspec = pl.BlockSpec((None, H_B, tq, HEAD_W), lambda bi, i: (bi, 0, i, 0))
    whole = lambda n: pl.BlockSpec((None, H_B, n, HEAD_W), lambda bi, i: (bi, 0, 0, 0))
    whole_ki = lambda n: pl.BlockSpec((None, n, D_I), lambda bi, i: (bi, 0, 0))
    args = [q, qi, wi]
    specs = [qspec, pl.BlockSpec((None, H_I, tq, D_I), lambda bi, i: (bi, 0, i, 0)),
             pl.BlockSpec((None, tq, LANE), lambda bi, i: (bi, i, 0))]
    p_len = 0
    if past is not None:
        p_len = past[0].shape[2]
        args += list(past)
        specs += [whole(p_len), whole(p_len), whole_ki(p_len)]
    args += [kn, vn, kin]
    specs += [whole(ln), whole(ln), whole_ki(ln)]
    kern = functools.partial(_dsa_attn_kernel, tq=tq, tkn=tkn, ln=ln, causal=causal, first_key=first_key,
                             p_len=p_len, tkp=KEY_TILE, k_sel=k_sel)
    return pl.pallas_call(
        kern, grid=(b, l // tq), in_specs=specs,
        out_specs=pl.BlockSpec((None, tq, H_B * DH_B), lambda bi, i: (bi, i, 0)),
        out_shape=jax.ShapeDtypeStruct((b, l, H_B * DH_B), F32),
        scratch_shapes=[pltpu.VMEM((tq, p_len + -(-ln // (2 * KEY_TILE)) * 2 * KEY_TILE), I32),
                        pltpu.VMEM((H_B, tq, p_len + ln), F32)],
        compiler_params=_cparams("parallel", "arbitrary"), name="dsa_attn")(*args)


def _softplus(x):
    return jnp.maximum(x, 0.0) + jnp.log(1.0 + jnp.exp(-jnp.abs(x)))


def _rwkv_prep_kernel(zc_ref, zp_ref, sh_ref, mu_ref, w0_ref, a0_ref, kkw_ref, ka_ref,
                      wup_ref, aup_ref, gup_ref, bd_ref,
                      r_ref, k_ref, v_ref, ld_ref, kk_ref, b_ref, g_ref, *, tl, lo, hi):
    i = pl.program_id(1)
    z = zc_ref[...]
    rowi = lax.broadcasted_iota(I32, (tl, 1), 0)
    prev_last = jnp.where(i == 0, sh_ref[...], zp_ref[7:8, :])
    zprev = jnp.where(rowi == 0, prev_last, pltpu.roll(z, 1, 0))
    zs = z + (zprev - z) * mu_ref[...]
    r, k, v, lora = zs[:, :W_C], zs[:, W_C:2 * W_C], zs[:, 2 * W_C:3 * W_C], zs[:, 3 * W_C:]
    lane = lax.broadcasted_iota(I32, (1, LANE), 1)
    u = jnp.where(lane < W_LORA, jnp.tanh(lora),
                  jnp.where(lane < W_LORA + A_LORA, lora, jax.nn.sigmoid(lora)))
    w = -_softplus(-(w0_ref[...] + _mm(u, wup_ref[...]))) - 0.5
    ld = -jnp.exp(w)
    a = jax.nn.sigmoid(a0_ref[...] + _mm(u, aup_ref[...]))
    g = _mm(u, gup_ref[...])
    kmod = k * (1.0 + (a - 1.0) * ka_ref[...])
    kk = k * kkw_ref[...]
    sq = kk * kk
    bd = bd_ref[...]
    if bd.dtype == F32:
        ss = _mm(sq, bd)
    else:
        sq_hi = sq.astype(BF16)
        ss = _mm(sq_hi, bd) + _mm(sq - sq_hi.astype(F32), bd)
    kkn = kk * jnp.minimum(lax.rsqrt(ss), 1e12)
    grow = i * tl + rowi
    valid = ((grow >= lo) & (grow < hi)).astype(F32)
    r_ref[...] = r
    k_ref[...] = kmod * valid
    v_ref[...] = v * valid
    ld_ref[...] = ld * valid
    kk_ref[...] = kkn * valid
    b_ref[...] = kkn * a * valid
    g_ref[...] = g


def _rwkv_prep(zc3, shift, p, tl, lo, hi):
    b, l, _ = zc3.shape
    rows = lambda w: pl.BlockSpec((None, tl, w), lambda bi, i: (bi, i, 0))
    vec = _full((1, W_C))
    mat = _full((LANE, W_C))
    kern = functools.partial(_rwkv_prep_kernel, tl=tl, lo=lo, hi=hi)
    return pl.pallas_call(
        kern, grid=(b, l // tl),
        in_specs=[rows(C_W),
                  pl.BlockSpec((None, 8, C_W), lambda bi, i: (bi, jnp.maximum(i * (tl // 8) - 1, 0), 0)),
                  pl.BlockSpec((None, 1, C_W), lambda bi, i: (bi, 0, 0)),
                  _full((1, C_W)), vec, vec, vec, vec, mat, mat, mat, _full((W_C, W_C))],
        out_specs=[rows(W_C)] * 7,
        out_shape=[jax.ShapeDtypeStruct((b, l, W_C), F32)] * 7,
        compiler_params=_cparams("parallel", "parallel"), name="rwkv_prep")(
            zc3, zc3, shift, p['mu'], p['w0'], p['a0'], p['k_k'], p['k_a'], p['wup'], p['aup'], p['gup'], p['bd'])


def _rwkv_scan_kernel(r_ref, k_ref, v_ref, ld_ref, kk_ref, b_ref, g_ref, s0_ref, lng_ref, lnb_ref, rk_ref,
                      o_ref, sout_ref, s_scr, *, c, n_chunks, md):
    ci = pl.program_id(1)

    @pl.when(ci == 0)
    def _():
        s_scr[...] = s0_ref[...]

    ld = ld_ref[...]
    rowi = lax.broadcasted_iota(I32, (c, 1), 0)
    cum = ld
    step = 1
    while step < c:
        cum = cum + jnp.where(rowi >= step, pltpu.roll(cum, step, 0), 0.0)
        step *= 2
    cl = cum[c - 1:c, :]
    e_in, e_ex, e_neg, e_end = jnp.exp(cum), jnp.exp(cum - ld), jnp.exp(-cum), jnp.exp(cl - cum)
    d_end = jnp.exp(cl)
    r, k, v, kk, bb = r_ref[...], k_ref[...], v_ref[...], kk_ref[...], b_ref[...]
    rt, kap, kt, bt = r * e_in, kk * e_ex, k * e_neg, bb * e_neg
    kte, bte = k * e_end, bb * e_end
    bonus_rk = r * k * rk_ref[...]
    ri = lax.broadcasted_iota(I32, (c, c), 0)
    cj = lax.broadcasted_iota(I32, (c, c), 1)
    strict, incl = ri > cj, ri >= cj
    eye = jnp.where(ri == cj, 1.0, 0.0)
    n_sq = c.bit_length() - 2
    heads = range(H_C)
    sls = [slice(h * N_C, (h + 1) * N_C) for h in heads]
    s_old = [s_scr[h] for h in heads]
    s_mm = [s.astype(md) for s in s_old]
    left = [jnp.concatenate([kap[:, sl], rt[:, sl]], axis=0).astype(md) for sl in sls]
    right = [jnp.concatenate([bt[:, sl], kt[:, sl]], axis=0).astype(md) for sl in sls]
    gram = [_mm_t(left[h], right[h]) for h in heads]
    a_kb = [jnp.where(strict, g[:c, :c], 0.0) for g in gram]
    a_kk = [jnp.where(strict, g[:c, c:], 0.0) for g in gram]
    a_rb = [jnp.where(incl, g[c:, :c], 0.0) for g in gram]
    a_rk = [jnp.where(incl, g[c:, c:], 0.0) for g in gram]
    vh = [v[:, sl] for sl in sls]
    vb = [x.astype(md) for x in vh]
    x0 = [_mm_t(left[h][:c], s_mm[h]) + _mm(a_kk[h], vb[h]) for h in heads]
    y0 = [_mm_t(left[h][c:], s_mm[h]) + _mm(a_rk[h], vb[h]) for h in heads]
    tinv = [eye - n for n in a_kb]
    npow = a_kb
    for _ in range(n_sq):
        nb = [n.astype(md) for n in npow]
        npow = [_mm(n, n) for n in nb]
        tinv = [tinv[h] + _mm(tinv[h], npow[h].astype(md)) for h in heads]
    w = [_mm(tinv[h], x0[h].astype(md)) for h in heads]
    wb = [x.astype(md) for x in w]
    y = [y0[h] - _mm(a_rb[h], wb[h]) for h in heads]
    for h in heads:
        sl = sls[h]
        s_scr[h] = (s_old[h] * d_end[:, sl] + _mm(vh[h].T, kte[:, sl].astype(md))
                    - _mm(w[h].T, bte[:, sl].astype(md)))
    for h in heads:
        sl = sls[h]
        ym = jnp.mean(y[h], axis=-1, keepdims=True)
        yc = y[h] - ym
        yn = yc * lax.rsqrt(jnp.mean(yc * yc, axis=-1, keepdims=True) + GN_EPS) * lng_ref[:, sl] + lnb_ref[:, sl]
        bonus = jnp.sum(bonus_rk[:, sl], axis=-1, keepdims=True) * vh[h]
        o_ref[:, sl] = (yn + bonus) * g_ref[:, sl]

    @pl.when(ci == n_chunks - 1)
    def _():
        sout_ref[...] = s_scr[...]


def _rwkv_scan(pre, s0, p, c, mm_dtype):
    r = pre[0]
    b, l, _ = r.shape
    n_chunks = l // c
    rows = pl.BlockSpec((None, c, W_C), lambda bi, i: (bi, i, 0))
    state = pl.BlockSpec((None, H_C, N_C, N_C), lambda bi, i: (bi, 0, 0, 0))
    vec = _full((1, W_C))
    kern = functools.partial(_rwkv_scan_kernel, c=c, n_chunks=n_chunks, md=mm_dtype)
    return pl.pallas_call(
        kern, grid=(b, n_chunks),
        in_specs=[rows] * 7 + [state, vec, vec, vec],
        out_specs=[rows, state],
        out_shape=[jax.ShapeDtypeStruct((b, l, W_C), F32), jax.ShapeDtypeStruct((b, H_C, N_C, N_C), F32)],
        scratch_shapes=[pltpu.VMEM((H_C, N_C, N_C), F32)],
        compiler_params=_cparams("parallel", "arbitrary"), name="rwkv_scan")(
            *pre, s0, p['ln_g'], p['ln_b'], p['r_k'])


def _layernorm(y, g, b):
    mu = jnp.mean(y, axis=-1, keepdims=True)
    yc = y - mu
    return yc * lax.rsqrt(jnp.mean(yc * yc, axis=-1, keepdims=True) + LN_EPS) * g + b


def _row_valid(i, tm, tiles_per_batch, lo, hi):
    rowb = (i % tiles_per_batch) * tm + lax.broadcasted_iota(I32, (tm, 1), 0)
    return (rowb >= lo) & (rowb < hi)


def _outproj_kernel(oa_ref, ob_ref, oc_ref, x_ref, wa_ref, wb_ref, wc_ref, g_ref, b_ref, y_ref,
                    *, tm, tiles_per_batch, lo, hi, masked):
    mix = _mm(oa_ref[...], wa_ref[...]) + _mm(ob_ref[...], wb_ref[...]) + _mm(oc_ref[...], wc_ref[...])
    y = _layernorm(ALPHA * x_ref[...] + mix, g_ref[...], b_ref[...])
    if masked:
        y = jnp.where(_row_valid(pl.program_id(0), tm, tiles_per_batch, lo, hi), y, 0.0)
    y_ref[...] = y


def _outproj(oa, ob, oc, x2d, p, tm, rows_per_batch, lo, hi, masked):
    t = x2d.shape[0]
    row = lambda w: pl.BlockSpec((tm, w), lambda i: (i, 0))
    kern = functools.partial(_outproj_kernel, tm=tm, tiles_per_batch=rows_per_batch // tm, lo=lo, hi=hi,
                             masked=masked)
    return pl.pallas_call(
        kern, grid=(t // tm,),
        in_specs=[row(oa.shape[1]), row(ob.shape[1]), row(oc.shape[1]), row(D_MODEL),
                  _full(p['wo_a'].shape), _full(p['wo_b'].shape), _full(p['wo_c'].shape),
                  _full((1, D_MODEL)), _full((1, D_MODEL))],
        out_specs=row(D_MODEL), out_shape=jax.ShapeDtypeStruct((t, D_MODEL), F32),
        compiler_params=_cparams("parallel"), name="outproj_ln")(
            oa, ob, oc, x2d, p['wo_a'], p['wo_b'], p['wo_c'], p['ln1_g'], p['ln1_b'])


def _router_kernel(x_ref, whi_ref, wlo_ref, br_ref, dg_ref):
    x = x_ref[...]
    xh = x.astype(BF16)
    xl = (x - xh.astype(F32)).astype(BF16)
    logits = _mm(xh, whi_ref[...]) + _mm(xl, whi_ref[...]) + _mm(xh, wlo_ref[...]) + br_ref[...]
    lane = lax.broadcasted_iota(I32, (1, LANE), 1).astype(F32)
    big = float(LANE)
    gm = (lane >= N_EXPERTS) & (lane < N_EXPERTS + N_GROUPS)
    gl = jnp.where(gm, logits, -jnp.inf)
    ge = jnp.exp(gl - jnp.max(gl, axis=1, keepdims=True))
    gp = ge / jnp.sum(ge, axis=1, keepdims=True)
    gval = jnp.max(gp, axis=1, keepdims=True)
    gidx = jnp.min(jnp.where(gm & (gp == gval), lane, big), axis=1, keepdims=True) - N_EXPERTS
    elo = gidx * EXP_PER_GROUP
    em = (lane >= elo) & (lane < elo + EXP_PER_GROUP)
    el = jnp.where(em, logits, -jnp.inf)
    ee = jnp.exp(el - jnp.max(el, axis=1, keepdims=True))
    ep = jnp.where(em, ee / jnp.sum(ee, axis=1, keepdims=True), -1.0)
    p1 = jnp.max(ep, axis=1, keepdims=True)
    i1 = jnp.min(jnp.where(ep == p1, lane, big), axis=1, keepdims=True)
    ep2 = jnp.where(lane == i1, -1.0, ep)
    p2 = jnp.max(ep2, axis=1, keepdims=True)
    i2 = jnp.min(jnp.where((ep2 == p2) & (lane != i1), lane, big), axis=1, keepdims=True)
    den = p1 + p2
    dg_ref[...] = jnp.where(lane == i1, gval * p1 / den, 0.0) + jnp.where(lane == i2, gval * p2 / den, 0.0)


def _router(x2d, p, tm):
    t = x2d.shape[0]
    row = lambda w: pl.BlockSpec((tm, w), lambda i: (i, 0))
    return pl.pallas_call(
        _router_kernel, grid=(t // tm,),
        in_specs=[row(D_MODEL), _full((D_MODEL, LANE)), _full((D_MODEL, LANE)), _full((1, LANE))],
        out_specs=row(LANE), out_shape=jax.ShapeDtypeStruct((t, LANE), F32),
        compiler_params=_cparams("parallel"), name="router")(x2d, p['wr_hi'], p['wr_lo'], p['br'])


def _moe_kernel(x_ref, dg_ref, wgu_ref, wd_ref, g_ref, b_ref, y_ref, acc_ref, xb_ref,
                *, tm, tiles_per_batch, lo, hi, masked):
    e = pl.program_id(1)

    @pl.when(e == 0)
    def _():
        xb_ref[...] = x_ref[...].astype(xb_ref.dtype)
        acc_ref[...] = jnp.zeros_like(acc_ref)

    hcat = _mm(xb_ref[...], wgu_ref[...])
    hg, hu = hcat[:, :D_EXPERT], hcat[:, D_EXPERT:]
    lane = lax.broadcasted_iota(I32, (1, LANE), 1)
    gate = jnp.sum(jnp.where(lane == e, dg_ref[...], 0.0), axis=1, keepdims=True)
    acc_ref[...] += _mm(hg * jax.nn.sigmoid(hg) * hu * gate, wd_ref[...])

    @pl.when(e == N_EXPERTS - 1)
    def _():
        y = _layernorm(ALPHA * x_ref[...] + acc_ref[...], g_ref[...], b_ref[...])
        if masked:
            y = jnp.where(_row_valid(pl.program_id(0), tm, tiles_per_batch, lo, hi), y, 0.0)
        y_ref[...] = y


def _moe(x2d, dg, p, tm, rows_per_batch, lo, hi, masked):
    t = x2d.shape[0]
    row = lambda w: pl.BlockSpec((tm, w), lambda i, e: (i, 0))
    kern = functools.partial(_moe_kernel, tm=tm, tiles_per_batch=rows_per_batch // tm, lo=lo, hi=hi,
                             masked=masked)
    return pl.pallas_call(
        kern, grid=(t // tm, N_EXPERTS),
        in_specs=[row(D_MODEL), row(LANE),
                  pl.BlockSpec((None, D_MODEL, 2 * D_EXPERT), lambda i, e: (e, 0, 0)),
                  pl.BlockSpec((None, D_EXPERT, D_MODEL), lambda i, e: (e, 0, 0)),
                  pl.BlockSpec((1, D_MODEL), lambda i, e: (0, 0)), pl.BlockSpec((1, D_MODEL), lambda i, e: (0, 0))],
        out_specs=row(D_MODEL), out_shape=jax.ShapeDtypeStruct((t, D_MODEL), F32),
        scratch_shapes=[pltpu.VMEM((tm, D_MODEL), F32), pltpu.VMEM((tm, D_MODEL), p['wgu'].dtype)],
        compiler_params=_cparams("parallel", "arbitrary"), name="moe_ln")(
            x2d, dg, p['wgu'], p['wd'], p['ln2_g'], p['ln2_b'])


def _rope_table(pos, pattern):
    posf = pos.astype(F32)[:, None]
    n = pos.shape[0]
    cos, s_lo, s_hi = [], [], []
    for kind, w in pattern:
        if kind == 'rope':
            half = w // 2
            inv = ROPE_THETA ** (-jnp.arange(half, dtype=F32) / half)
            ang = posf * inv[None, :]
            c, s, z = jnp.cos(ang), jnp.sin(ang), jnp.zeros((n, half), F32)
            cos += [c, c]
            s_lo += [-s, z]
            s_hi += [z, s]
        else:
            fill = jnp.full((n, w), 1.0 if kind == 'one' else 0.0, F32)
            z = jnp.zeros((n, w), F32)
            cos.append(fill)
            s_lo.append(z)
            s_hi.append(z)
    return jnp.stack([jnp.concatenate(t, axis=1) for t in (cos, s_lo, s_hi)])


def _tables(pos):
    return {
        'mla_q': _rope_table(pos, [('one', NOPE_A), ('rope', ROPE_A), ('zero', HEAD_W - NOPE_A - ROPE_A)]),
        'mla_kr': _rope_table(pos, [('rope', ROPE_A), ('zero', LANE - ROPE_A)]),
        'dsa_qk': _rope_table(pos, [('rope', DH_B), ('zero', HEAD_W - DH_B)]),
        'idx_q': _rope_table(pos, [('rope', D_I)] * (LANE // D_I)),
        'idx_k': _rope_table(pos, [('rope', D_I), ('one', H_I), ('zero', LANE - D_I - H_I)]),
    }


def _pad_cols(w, width):
    return jnp.pad(w, [(0, 0)] * (w.ndim - 1) + [(0, width - w.shape[-1])])


def _head_pad(w, n_heads, width):
    d = w.shape[-1] // n_heads
    w = w.reshape(w.shape[:-1] + (n_heads, d))
    return _pad_cols(w, width).reshape(w.shape[:-2] + (n_heads * width,))


def _prepare_weights(w, md):
    w_in = w['w_in']
    a, bseg, cseg = w_in[..., :A_COLS], w_in[..., A_COLS:A_COLS + B_COLS], w_in[..., A_COLS + B_COLS:]
    hd = H_B * DH_B
    wb = jnp.concatenate([
        _head_pad(bseg[..., 0:hd], H_B, HEAD_W), _head_pad(bseg[..., hd:2 * hd], H_B, HEAD_W),
        _head_pad(bseg[..., 2 * hd:3 * hd], H_B, HEAD_W), bseg[..., 3 * hd:3 * hd + H_I * D_I],
        _pad_cols(bseg[..., 3 * hd + H_I * D_I:], LANE)], axis=-1)
    w_uq = w['mla_w_uq'].reshape(DEPTH, Q_LORA, H_A, NOPE_A + ROPE_A)
    w_ukv = w['mla_w_ukv'].reshape(DEPTH, KV_LORA, H_A, NOPE_A + V_A)
    zeros_lora = lambda n: jnp.zeros((DEPTH, n, W_C), F32)
    head_of = jnp.arange(W_C) // N_C
    wr = jnp.concatenate([w['router_expert'], w['router_group']], axis=-1)
    wr = _pad_cols(wr, LANE)
    wr_hi = wr.astype(BF16)
    row = lambda v: v.reshape(DEPTH, 1, -1)
    return {
        'wa': _pad_cols(a, A_W).astype(md), 'wb': wb.astype(md), 'wc': cseg.astype(md),
        'qg': row(w['mla_q_norm']), 'kg': row(w['mla_kv_norm']),
        'wuq': _pad_cols(w_uq, HEAD_W).reshape(DEPTH, Q_LORA, H_A * HEAD_W).astype(md),
        'wk': _pad_cols(w_ukv[..., :NOPE_A], HEAD_W).reshape(DEPTH, KV_LORA, H_A * HEAD_W).astype(md),
        'wv': _pad_cols(w_ukv[..., NOPE_A:], HEAD_W).reshape(DEPTH, KV_LORA, H_A * HEAD_W).astype(md),
        'mu': row(w['rwkv_mu']), 'w0': row(w['rwkv_w0']), 'a0': row(w['rwkv_a0']),
        'k_k': row(w['rwkv_k_k']), 'k_a': row(w['rwkv_k_a']), 'r_k': row(w['rwkv_r_k']),
        'ln_g': row(w['rwkv_ln_g']), 'ln_b': row(w['rwkv_ln_b']),
        'wup': jnp.concatenate([w['rwkv_w_up'], zeros_lora(LANE - W_LORA)], axis=1).astype(md),
        'aup': jnp.concatenate([zeros_lora(W_LORA), w['rwkv_a_up'], zeros_lora(G_LORA)], axis=1).astype(md),
        'gup': jnp.concatenate([zeros_lora(W_LORA + A_LORA), w['rwkv_g_up']], axis=1).astype(md),
        'bd': jnp.broadcast_to((head_of[:, None] == head_of[None, :]).astype(md), (DEPTH, W_C, W_C)),
        'wo_a': w['w_out'][:, :H_A * V_A].astype(md),
        'wo_b': w['w_out'][:, H_A * V_A:H_A * V_A + hd].astype(md),
        'wo_c': w['w_out'][:, H_A * V_A + hd:].astype(md),
        'ln1_g': row(w['ln1_g']), 'ln1_b': row(w['ln1_b']), 'ln2_g': row(w['ln2_g']), 'ln2_b': row(w['ln2_b']),
        'wr_hi': wr_hi, 'wr_lo': (wr - wr_hi.astype(F32)).astype(BF16),
        'br': _pad_cols(jnp.concatenate([w['router_expert_b'], w['router_group_b']], axis=-1), LANE).reshape(DEPTH, 1, LANE),
        'wgu': jnp.concatenate([w['exp_w_gate'], w['exp_w_up']], axis=-1).astype(md),
        'wd': w['exp_w_down'].astype(md),
    }


def _largest_tile(n, limit):
    best = 8
    for t in range(8, min(n, limit) + 1, 8):
        if n % t == 0:
            best = t
    return best


def _trunk(x, pos, past, wp, *, lo, hi, causal, k_sel):
    b, l, _ = x.shape
    t = b * l
    tabs = _tables(pos)
    tl = min(256, l)
    tq_mla = min(2 * CHUNK, l)
    tq_dsa = min(2 * CHUNK, l)
    c_rwkv = min(CHUNK, l)
    masked = not (lo == 0 and hi == l)
    md = wp['wa'].dtype
    tm_cap = 512 if md == BF16 else 128
    tm = _largest_tile(l, tm_cap) if masked else _largest_tile(t, tm_cap)
    tm_moe = _largest_tile(l, 1100) if masked else _largest_tile(t, 1100)
    first_key = lo if causal else 0
    new = {k: [] for k in ('ckv', 'krope', 'dsa_k', 'dsa_v', 'dsa_kidx', 'rwkv', 'shift')}
    x2d = x.reshape(t, D_MODEL)
    for layer in range(DEPTH):
        p = {k: v[layer] for k, v in wp.items()}
        za, zb, zc = _proj_in(x2d, p['wa'], p['wb'], p['wc'], tm)
        za3, zb3, zc3 = za.reshape(b, l, A_W), zb.reshape(b, l, B_W), zc.reshape(b, l, C_W)

        q, kc, v, ckv_new, kr_new = _mla_prep(za3, tabs['mla_q'], tabs['mla_kr'], p['qg'], p['kg'],
                                              p['wuq'], p['wk'], p['wv'], tl)
        mla_past = None
        if past is not None:
            mla_past = _mla_past(past['ckv'][layer], _pad_cols(past['krope'][layer], LANE), p['wk'], p['wv'], 512)
        oa = _mla_attn(q, kc, v, mla_past, tq_mla, causal, first_key)

        qd, kd, vd, qi, ki, wi, kst, vst, kist = _dsa_prep(zb3, tabs['dsa_qk'], tabs['idx_q'], tabs['idx_k'], tl, md)
        dsa_past = None
        if past is not None:
            to_heads = lambda c: _pad_cols(jnp.swapaxes(c.astype(md), 1, 2), HEAD_W)
            ones_col = (jnp.arange(HEAD_W) == DH_B).astype(md)
            dsa_past = (to_heads(past['dsa_k'][layer]), to_heads(past['dsa_v'][layer]) + ones_col,
                        past['dsa_kidx'][layer].astype(md))
        ob = _dsa_attn(qd, qi, wi, kd, vd, ki, dsa_past, tq_dsa, causal, first_key, k_sel)

        if past is not None:
            shift, s0 = past['shift'][layer][:, None, :], past['rwkv'][layer]
        else:
            shift, s0 = jnp.zeros((b, 1, C_W), F32), jnp.zeros((b, H_C, N_C, N_C), F32)
        pre = _rwkv_prep(zc3, shift, p, tl, lo, hi)
        oc, s_last = _rwkv_scan(pre, s0, p, c_rwkv, md)

        x1 = _outproj(oa.reshape(t, -1), ob.reshape(t, -1), oc.reshape(t, -1), x2d, p, tm, l, lo, hi, masked)
        dg = _router(x1, p, tm)
        x2d = _moe(x1, dg, p, tm_moe, l, lo, hi, masked)

        n = hi - lo
        new['ckv'].append(ckv_new[:, lo:hi])
        new['krope'].append(kr_new[:, lo:hi])
        new['dsa_k'].append(kst[:, lo:hi].reshape(b, n, H_B, DH_B))
        new['dsa_v'].append(vst[:, lo:hi].reshape(b, n, H_B, DH_B))
        new['dsa_kidx'].append(kist[:, lo:hi])
        new['rwkv'].append(s_last)
        new['shift'].append(zc3[:, hi - 1, :])
    return x2d.reshape(b, l, D_MODEL), {k: jnp.stack(v) for k, v in new.items()}


def kernel(x_prompt, x_sample, cache_mla_ckv, cache_mla_krope, cache_dsa_k, cache_dsa_v, cache_dsa_kidx, state_rwkv, state_rwkv_shift, meta_tokens, w_in, mla_q_norm, mla_kv_norm, mla_w_uq, mla_w_ukv, rwkv_mu, rwkv_w0, rwkv_w_up, rwkv_a0, rwkv_a_up, rwkv_g_up, rwkv_k_k, rwkv_k_a, rwkv_r_k, rwkv_ln_g, rwkv_ln_b, w_out, ln1_g, ln1_b, ln2_g, ln2_b, router_group, router_group_b, router_expert, router_expert_b, exp_w_gate, exp_w_up, exp_w_down):
    weights = {
        'w_in': w_in, 'mla_q_norm': mla_q_norm, 'mla_kv_norm': mla_kv_norm, 'mla_w_uq': mla_w_uq,
        'mla_w_ukv': mla_w_ukv, 'rwkv_mu': rwkv_mu, 'rwkv_w0': rwkv_w0, 'rwkv_w_up': rwkv_w_up,
        'rwkv_a0': rwkv_a0, 'rwkv_a_up': rwkv_a_up, 'rwkv_g_up': rwkv_g_up, 'rwkv_k_k': rwkv_k_k,
        'rwkv_k_a': rwkv_k_a, 'rwkv_r_k': rwkv_r_k, 'rwkv_ln_g': rwkv_ln_g, 'rwkv_ln_b': rwkv_ln_b,
        'w_out': w_out, 'ln1_g': ln1_g, 'ln1_b': ln1_b, 'ln2_g': ln2_g, 'ln2_b': ln2_b,
        'router_group': router_group, 'router_group_b': router_group_b, 'router_expert': router_expert,
        'router_expert_b': router_expert_b, 'exp_w_gate': exp_w_gate, 'exp_w_up': exp_w_up,
        'exp_w_down': exp_w_down,
    }
    wp = _prepare_weights(weights, BF16)
    wp_sample = _prepare_weights(weights, F32)

    bp, seq, _ = x_prompt.shape
    lo, hi = FRONT_PAD, FRONT_PAD + N_META + seq
    lp = -(-hi // KEY_TILE) * KEY_TILE
    meta = jnp.broadcast_to(meta_tokens.astype(F32)[None], (bp, N_META, D_MODEL))
    xp = jnp.concatenate([jnp.zeros((bp, lo, D_MODEL), F32), meta, x_prompt,
                          jnp.zeros((bp, lp - hi, D_MODEL), F32)], axis=1)
    pos_p = jnp.arange(lp, dtype=jnp.int32) - lo
    yp, new_p = _trunk(xp, pos_p, None, wp, lo=lo, hi=hi, causal=True, k_sel=min(TOPK_MAX, seq // 4))
    y_prompt = yp[:, lo + N_META:hi]

    n_past, n_new = cache_mla_ckv.shape[2], x_sample.shape[1]
    assert n_past % KEY_TILE == 0 and n_new <= CHUNK and n_new % 8 == 0, (n_past, n_new)
    pos_s = n_past + jnp.arange(n_new, dtype=jnp.int32)
    past_s = {'ckv': cache_mla_ckv, 'krope': cache_mla_krope, 'dsa_k': cache_dsa_k, 'dsa_v': cache_dsa_v,
              'dsa_kidx': cache_dsa_kidx, 'rwkv': state_rwkv, 'shift': state_rwkv_shift}
    y_sample, new_s = _trunk(x_sample, pos_s, past_s, wp_sample, lo=0, hi=n_new, causal=False,
                             k_sel=min(TOPK_MAX, (n_past + n_new) // 4))

    keys = ('ckv', 'krope', 'dsa_k', 'dsa_v', 'dsa_kidx', 'rwkv', 'shift')
    return (y_prompt, y_sample) + tuple(new_p[k] for k in keys) + tuple(new_s[k] for k in keys)
```

```python
import functools

import jax
import jax.numpy as jnp
from jax import lax
from jax.experimental import pallas as pl
from jax.experimental.pallas import tpu as pltpu

F32 = jnp.float32
BF16 = jnp.bfloat16
I32 = jnp.int32

D_MODEL = 1024
DEPTH = 4
CHUNK = 64
N_META = 16
ROPE_THETA = 10000.0
NEG_INF = -1e30
LN_EPS = 1e-5
RMS_EPS = 1e-6
GN_EPS = 64e-5
ALPHA = (2 * DEPTH) ** 0.25

H_A, Q_LORA, KV_LORA, NOPE_A, ROPE_A, V_A = 4, 256, 128, 64, 32, 64
H_B, DH_B, H_I, D_I, TOPK_MAX = 4, 64, 8, 32, 256
H_C, N_C, W_C, W_LORA, A_LORA, G_LORA = 8, 64, 512, 32, 32, 64
N_GROUPS, EXP_PER_GROUP, N_EXPERTS, D_EXPERT = 4, 4, 16, 256

A_COLS = Q_LORA + KV_LORA + ROPE_A
B_COLS = 3 * H_B * DH_B + H_I * D_I + D_I + H_I
C_COLS = 3 * W_C + W_LORA + A_LORA + G_LORA

MLA_SCALE = (NOPE_A + ROPE_A) ** -0.5
DSA_SCALE = DH_B ** -0.5
IDX_SCALE = (H_I ** -0.5) * (D_I ** -0.5)

LANE = 128
HEAD_W = 128
A_W = Q_LORA + KV_LORA + LANE
B_Q, B_K, B_V = 0, H_B * HEAD_W, 2 * H_B * HEAD_W
B_QI = 3 * H_B * HEAD_W
B_KI = B_QI + H_I * D_I
B_W = B_KI + LANE
C_W = C_COLS
FRONT_PAD = CHUNK - N_META
KEY_TILE = 256
TILE_UNROLL = 2
INT_MIN = -2147483648
INT_MAX = 2147483647
VMEM_LIMIT = 56 * 1024 * 1024


def _cparams(*sem):
    return pltpu.CompilerParams(dimension_semantics=sem, vmem_limit_bytes=VMEM_LIMIT)


def _full(shape):
    nd = len(shape)
    return pl.BlockSpec(shape, lambda *_: (0,) * nd)


def _round_up(n, m):
    return -(-n // m) * m


def _tile_lanes(t, reps):
    return t if reps == 1 else jnp.concatenate([t] * reps, axis=1)


def _rope(x, tab_ref, half):
    n = x.shape[1]
    reps = n // LANE
    cos = _tile_lanes(tab_ref[0], reps)
    sin_lo = _tile_lanes(tab_ref[1], reps)
    sin_hi = _tile_lanes(tab_ref[2], reps)
    return x * cos + pltpu.roll(x, n - half, 1) * sin_lo + pltpu.roll(x, half, 1) * sin_hi


def _mm(a, b):
    if b.dtype == F32:
        return jnp.dot(a.astype(F32), b, preferred_element_type=F32, precision=lax.Precision.HIGHEST)
    return jnp.dot(a.astype(BF16), b, preferred_element_type=F32)


def _mm_t(a, b):
    prec = lax.Precision.HIGHEST if b.dtype == F32 else None
    return lax.dot_general(a.astype(b.dtype), b, (((1,), (1,)), ((), ())), preferred_element_type=F32,
                           precision=prec)


def _pair_heads(parts):
    lo = parts[0] + pltpu.roll(parts[1], 64, 1)
    hi = parts[2] + pltpu.roll(parts[3], 64, 1)
    return jnp.concatenate([lo, hi], axis=1)


def _proj_in_kernel(x_ref, wa_ref, wb_ref, wc_ref, za_ref, zb_ref, zc_ref):
    xb = x_ref[...].astype(wa_ref.dtype)
    za_ref[...] = _mm(xb, wa_ref[...])
    zb_ref[...] = _mm(xb, wb_ref[...])
    zc_ref[...] = _mm(xb, wc_ref[...])


def _proj_in(x2d, wa, wb, wc, tm):
    t = x2d.shape[0]
    row = lambda w: pl.BlockSpec((tm, w), lambda i: (i, 0))
    return pl.pallas_call(
        _proj_in_kernel, grid=(t // tm,),
        in_specs=[row(D_MODEL), _full(wa.shape), _full(wb.shape), _full(wc.shape)],
        out_specs=[row(A_W), row(B_W), row(C_W)],
        out_shape=[jax.ShapeDtypeStruct((t, w), F32) for w in (A_W, B_W, C_W)],
        compiler_params=_cparams("parallel"), name="proj_in")(x2d, wa, wb, wc)


def _rms(x, g):
    return x * lax.rsqrt(jnp.mean(x * x, axis=-1, keepdims=True) + RMS_EPS) * g


def _mla_prep_kernel(za_ref, tq_ref, tk_ref, qg_ref, kg_ref, wuq_ref, wk_ref, wv_ref,
                     q_ref, kc_ref, v_ref, ckv_ref, kr_ref):
    za = za_ref[...]
    qn = _rms(za[:, :Q_LORA], qg_ref[...])
    q = _rope(_mm(qn, wuq_ref[...]), tq_ref, ROPE_A // 2) * MLA_SCALE
    ckvn = _rms(za[:, Q_LORA:Q_LORA + KV_LORA], kg_ref[...])
    ckv_ref[...] = ckvn
    kr = _rope(za[:, Q_LORA + KV_LORA:], tk_ref, ROPE_A // 2)
    kr_ref[...] = kr[:, :ROPE_A]
    kc = _mm(ckvn, wk_ref[...]) + _tile_lanes(pltpu.roll(kr, NOPE_A, 1), H_A)
    v = _mm(ckvn, wv_ref[...]) + _ones_lane(H_A * HEAD_W)
    for h in range(H_A):
        sl = slice(h * HEAD_W, (h + 1) * HEAD_W)
        q_ref[h] = q[:, sl].astype(q_ref.dtype)
        kc_ref[h] = kc[:, sl].astype(kc_ref.dtype)
        v_ref[h] = v[:, sl].astype(v_ref.dtype)


def _mla_prep(za3, tq, tk, qg, kg, wuq, wk, wv, tl):
    b, l, _ = za3.shape
    heads = pl.BlockSpec((None, H_A, tl, HEAD_W), lambda bi, i: (bi, 0, i, 0))
    tab = pl.BlockSpec((3, tl, LANE), lambda bi, i: (0, i, 0))
    hshape = jax.ShapeDtypeStruct((b, H_A, l, HEAD_W), wuq.dtype)
    return pl.pallas_call(
        _mla_prep_kernel, grid=(b, l // tl),
        in_specs=[pl.BlockSpec((None, tl, A_W), lambda bi, i: (bi, i, 0)), tab, tab,
                  _full(qg.shape), _full(kg.shape), _full(wuq.shape), _full(wk.shape), _full(wv.shape)],
        out_specs=[heads, heads, heads,
                   pl.BlockSpec((None, tl, KV_LORA), lambda bi, i: (bi, i, 0)),
                   pl.BlockSpec((None, tl, ROPE_A), lambda bi, i: (bi, i, 0))],
        out_shape=[hshape, hshape, hshape,
                   jax.ShapeDtypeStruct((b, l, KV_LORA), F32), jax.ShapeDtypeStruct((b, l, ROPE_A), F32)],
        compiler_params=_cparams("parallel", "parallel"), name="mla_prep")(za3, tq, tk, qg, kg, wuq, wk, wv)


def _mla_past_kernel(ckv_ref, kr_ref, wk_ref, wv_ref, kc_ref, v_ref):
    cb = ckv_ref[...]
    kc = _mm(cb, wk_ref[...]) + _tile_lanes(pltpu.roll(kr_ref[...], NOPE_A, 1), H_A)
    v = _mm(cb, wv_ref[...]) + _ones_lane(H_A * HEAD_W)
    for h in range(H_A):
        sl = slice(h * HEAD_W, (h + 1) * HEAD_W)
        kc_ref[h] = kc[:, sl].astype(kc_ref.dtype)
        v_ref[h] = v[:, sl].astype(v_ref.dtype)


def _mla_past(ckv, kr128, wk, wv, tl):
    b, p, _ = ckv.shape
    heads = pl.BlockSpec((None, H_A, tl, HEAD_W), lambda bi, i: (bi, 0, i, 0))
    hshape = jax.ShapeDtypeStruct((b, H_A, p, HEAD_W), wk.dtype)
    return pl.pallas_call(
        _mla_past_kernel, grid=(b, p // tl),
        in_specs=[pl.BlockSpec((None, tl, KV_LORA), lambda bi, i: (bi, i, 0)),
                  pl.BlockSpec((None, tl, LANE), lambda bi, i: (bi, i, 0)), _full(wk.shape), _full(wv.shape)],
        out_specs=[heads, heads], out_shape=[hshape, hshape],
        compiler_params=_cparams("parallel", "parallel"), name="mla_past")(ckv, kr128, wk, wv)


def _ones_lane(width):
    lane = lax.broadcasted_iota(I32, (1, width), 1)
    return jnp.where((lane & (HEAD_W - 1)) == V_A, 1.0, 0.0)


def _fold_lanes_max(s):
    out = s[:, :LANE]
    for c0 in range(LANE, s.shape[1], LANE):
        out = jnp.maximum(out, s[:, c0:c0 + LANE])
    return out


def _tile_groups(nt, masked):
    if isinstance(nt, int):
        unroll = TILE_UNROLL if nt % TILE_UNROLL == 0 else 1
        return nt // unroll, unroll
    assert masked, "a rounded-up tile count needs a mask"
    return (nt + TILE_UNROLL - 1) // TILE_UNROLL, TILE_UNROLL


def _group_offsets(g, u, unroll, tk, last_tile):
    j = g * unroll + u
    off = pl.multiple_of(j * tk, tk)
    if unroll == 1:
        return off, off
    return off, pl.multiple_of(jnp.minimum(j, last_tile) * tk, tk)


def _attend(q_ref, sources, s_scr, tq, n_heads, mask_fn):
    heads = range(n_heads)
    qs = [q_ref[h] for h in heads]
    mrun = tuple(jnp.full((tq, LANE), NEG_INF, F32) for _ in heads)
    for (k_ref, _, base, tk, nt, masked) in sources:
        groups, unroll = _tile_groups(nt, masked)
        last = k_ref.shape[1] // tk - 1

        def score_body(g, ms, k_ref=k_ref, base=base, tk=tk, masked=masked, unroll=unroll, last=last):
            for u in range(unroll):
                off, kv_off = _group_offsets(g, u, unroll, tk, last)
                col = pl.multiple_of(base + off, tk)
                ss = [_mm_t(qs[h], k_ref[h, pl.ds(kv_off, tk), :]) for h in heads]
                if masked:
                    mask = mask_fn(off, col, tk)
                    ss = [jnp.where(mask, s, NEG_INF) for s in ss]
                for h in heads:
                    s_scr[h, :, pl.ds(col, tk)] = ss[h]
                if tk % LANE == 0:
                    ms = tuple(jnp.maximum(ms[h], _fold_lanes_max(ss[h])) for h in heads)
                else:
                    ms = tuple(jnp.maximum(ms[h], jnp.max(ss[h], axis=1, keepdims=True)) for h in heads)
            return ms
        mrun = lax.fori_loop(0, groups, score_body, mrun)
    m = [jnp.max(mr, axis=1, keepdims=True) for mr in mrun]
    accs = tuple(jnp.zeros((tq, HEAD_W), F32) for _ in heads)
    for (_, v_ref, base, tk, nt, masked) in sources:
        groups, unroll = _tile_groups(nt, masked)
        last = v_ref.shape[1] // tk - 1

        def pv_body(g, acc, v_ref=v_ref, base=base, tk=tk, unroll=unroll, last=last):
            for u in range(unroll):
                off, kv_off = _group_offsets(g, u, unroll, tk, last)
                col = pl.multiple_of(base + off, tk)
                ps = [jnp.exp(s_scr[h, :, pl.ds(col, tk)] - m[h]) for h in heads]
                acc = tuple(acc[h] + _mm(ps[h], v_ref[h, pl.ds(kv_off, tk), :]) for h in heads)
            return acc
        accs = lax.fori_loop(0, groups, pv_body, accs)
    lane = lax.broadcasted_iota(I32, (1, HEAD_W), 1)
    return [jnp.where(lane < V_A, a / a[:, V_A:V_A + 1], 0.0) for a in accs]


def _mla_attn_kernel(*refs, tq, tkn, ln, causal, first_key, p_len, tkp):
    if p_len:
        q_ref, kp_ref, vp_ref, kn_ref, vn_ref, o_ref, s_scr = refs
    else:
        q_ref, kn_ref, vn_ref, o_ref, s_scr = refs
    qt = pl.program_id(1)
    if causal:
        row = qt * tq + lax.broadcasted_iota(I32, (tq, 1), 0)
        nvis = ((row >> 6) + 1) << 6
        n_tiles = (qt * tq + tq + tkn - 1) // tkn
    else:
        nvis = None
        n_tiles = ln // tkn
    sources = []
    if p_len:
        sources.append((kp_ref, vp_ref, 0, tkp, p_len // tkp, False))
    sources.append((kn_ref, vn_ref, p_len, tkn, n_tiles, causal))

    def mask_fn(off, col, tk):
        key = off + lax.broadcasted_iota(I32, (1, tk), 1)
        return (key >= first_key) & (key < nvis)

    o_ref[...] = _pair_heads(_attend(q_ref, sources, s_scr, tq, H_A, mask_fn))


def _mla_attn(q, kn, vn, past, tq, causal, first_key):
    b, _, l, _ = q.shape
    ln = kn.shape[2]
    tkn = KEY_TILE if ln % KEY_TILE == 0 else ln
    qspec = pl.BlockSpec((None, H_A, tq, HEAD_W), lambda bi, i: (bi, 0, i, 0))
    whole = lambda n: pl.BlockSpec((None, H_A, n, HEAD_W), lambda bi, i: (bi, 0, 0, 0))
    args, specs, p_len = [q], [qspec], 0
    if past is not None:
        p_len = past[0].shape[2]
        args += list(past)
        specs += [whole(p_len), whole(p_len)]
    args += [kn, vn]
    specs += [whole(ln), whole(ln)]
    kern = functools.partial(_mla_attn_kernel, tq=tq, tkn=tkn, ln=ln, causal=causal, first_key=first_key,
                             p_len=p_len, tkp=KEY_TILE)
    return pl.pallas_call(
        kern, grid=(b, l // tq), in_specs=specs,
        out_specs=pl.BlockSpec((None, tq, H_A * V_A), lambda bi, i: (bi, i, 0)),
        out_shape=jax.ShapeDtypeStruct((b, l, H_A * V_A), F32),
        scratch_shapes=[pltpu.VMEM((H_A, tq, p_len + _round_up(ln, TILE_UNROLL * tkn)), F32)],
        compiler_params=_cparams("parallel", "arbitrary"), name="mla_attn")(*args)


def _dsa_prep_kernel(zb_ref, tqk_ref, tiq_ref, tik_ref,
                     q_ref, k_ref, v_ref, qi_ref, ki_ref, wi_ref, kst_ref, vst_ref, kist_ref):
    zb = zb_ref[...]
    q = _rope(zb[:, B_Q:B_K], tqk_ref, DH_B // 2) * DSA_SCALE
    k = _rope(zb[:, B_K:B_V], tqk_ref, DH_B // 2)
    v = zb[:, B_V:B_QI]
    qi = _rope(zb[:, B_QI:B_KI], tiq_ref, D_I // 2)
    kiw = _rope(zb[:, B_KI:B_W], tik_ref, D_I // 2)
    ks, vs = [], []
    for h in range(H_B):
        sl = slice(h * HEAD_W, (h + 1) * HEAD_W)
        q_ref[h] = q[:, sl].astype(q_ref.dtype)
        k_ref[h] = k[:, sl].astype(k_ref.dtype)
        v_ref[h] = (v[:, sl] + _ones_lane(HEAD_W)).astype(v_ref.dtype)
        ks.append(k[:, sl])
        vs.append(v[:, sl])
    for h in range(H_I):
        qi_ref[h] = qi[:, h * D_I:(h + 1) * D_I].astype(qi_ref.dtype)
    ki_ref[...] = kiw[:, :D_I].astype(ki_ref.dtype)
    wi_ref[...] = kiw * IDX_SCALE
    kst_ref[...] = _pair_heads(ks)
    vst_ref[...] = _pair_heads(vs)
    kist_ref[...] = kiw[:, :D_I]


def _dsa_prep(zb3, tqk, tiq, tik, tl, mm_dtype):
    b, l, _ = zb3.shape
    heads = pl.BlockSpec((None, H_B, tl, HEAD_W), lambda bi, i: (bi, 0, i, 0))
    tab = pl.BlockSpec((3, tl, LANE), lambda bi, i: (0, i, 0))
    rows = lambda w: pl.BlockSpec((None, tl, w), lambda bi, i: (bi, i, 0))
    hshape = jax.ShapeDtypeStruct((b, H_B, l, HEAD_W), mm_dtype)
    return pl.pallas_call(
        _dsa_prep_kernel, grid=(b, l // tl),
        in_specs=[rows(B_W), tab, tab, tab],
        out_specs=[heads, heads, heads,
                   pl.BlockSpec((None, H_I, tl, D_I), lambda bi, i: (bi, 0, i, 0)),
                   rows(D_I), rows(LANE), rows(H_B * DH_B), rows(H_B * DH_B), rows(D_I)],
        out_shape=[hshape, hshape, hshape,
                   jax.ShapeDtypeStruct((b, H_I, l, D_I), mm_dtype),
                   jax.ShapeDtypeStruct((b, l, D_I), mm_dtype),
                   jax.ShapeDtypeStruct((b, l, LANE), F32),
                   jax.ShapeDtypeStruct((b, l, H_B * DH_B), F32),
                   jax.ShapeDtypeStruct((b, l, H_B * DH_B), F32),
                   jax.ShapeDtypeStruct((b, l, D_I), F32)],
        compiler_params=_cparams("parallel", "parallel"), name="dsa_prep")(zb3, tqk, tiq, tik)


def _order_key(x):
    x = jnp.where(x == 0.0, 0.0, x)
    b = pltpu.bitcast(x, I32)
    return jnp.where(b < 0, b ^ INT_MAX, b)


def _dsa_attn_kernel(*refs, tq, tkn, ln, causal, first_key, p_len, tkp, k_sel):
    if p_len:
        (q_ref, qi_ref, wi_ref, kp_ref, vp_ref, kip_ref, kn_ref, vn_ref, kin_ref, o_ref,
         keys_ref, s_scr, wrep_scr) = refs
    else:
        (q_ref, qi_ref, wi_ref, kn_ref, vn_ref, kin_ref, o_ref, keys_ref, s_scr, wrep_scr) = refs
    qt = pl.program_id(1)
    p_tiles = p_len // tkp if p_len else 0
    if causal:
        row = qt * tq + lax.broadcasted_iota(I32, (tq, 1), 0)
        nvis = ((row >> 6) + 1) << 6
        n_tiles = (qt * tq + tq + tkn - 1) // tkn
        n_valid = qt * tq + tq - first_key + p_len
    else:
        nvis = ln
        n_tiles = ln // tkn
        n_valid = ln + p_len

    sources = []
    if p_len:
        sources.append((kp_ref, vp_ref, kip_ref, 0, tkp, p_tiles, False))
    sources.append((kn_ref, vn_ref, kin_ref, p_len, tkn, n_tiles, causal))

    def col_mask(off, tk):
        col = off + lax.broadcasted_iota(I32, (1, tk), 1)
        return (col >= first_key) & (col < nvis)

    wi = wi_ref[...]
    qis = [qi_ref[h] for h in range(H_I)]
    for h in range(H_I):
        wrep_scr[h] = jnp.broadcast_to(wi[:, D_I + h:D_I + h + 1], (tq, LANE))
    for (_, _, ki_ref, base, tk, nt, masked) in sources:
        groups, unroll = _tile_groups(nt, masked)
        last = ki_ref.shape[0] // tk - 1

        def score_body(g, c, ki_ref=ki_ref, base=base, tk=tk, masked=masked, unroll=unroll, last=last):
            wts = [_tile_lanes(wrep_scr[h], tk // LANE) if tk >= LANE else wrep_scr[h][:, :tk] for h in range(H_I)]
            for u in range(unroll):
                off, kv_off = _group_offsets(g, u, unroll, tk, last)
                ki_t = ki_ref[pl.ds(kv_off, tk), :]
                sc = jnp.maximum(_mm_t(qis[0], ki_t), 0.0) * wts[0]
                for h in range(1, H_I):
                    sc = sc + jnp.maximum(_mm_t(qis[h], ki_t), 0.0) * wts[h]
                key = _order_key(sc)
                if masked:
                    key = jnp.where(col_mask(off, tk), key, INT_MIN)
                keys_ref[:, pl.ds(pl.multiple_of(base + off, tk), tk)] = key
            return c
        lax.fori_loop(0, groups, score_body, 0)

    count_sources = []
    for (_, _, _, base, tk, nt, _) in sources:
        if tk == KEY_TILE:
            count_sources.append((base, TILE_UNROLL * tk, _tile_groups(nt, True)[0] if not isinstance(nt, int)
                                  else nt // TILE_UNROLL))
        else:
            count_sources.append((base, tk, nt))

    def count_ge(mid):
        total = jnp.zeros((tq, 1), F32)
        for (base, tk, nt) in count_sources:
            wacc = min(tk, LANE)

            def cnt_body(j, acc, base=base, tk=tk, wacc=wacc):
                off = pl.multiple_of(base + j * tk, tk)
                hit = jnp.where(keys_ref[:, pl.ds(off, tk)] >= mid, 1.0, 0.0)
                for c0 in range(0, tk, wacc):
                    acc = acc + hit[:, c0:c0 + wacc]
                return acc
            acc = lax.fori_loop(0, nt, cnt_body, jnp.zeros((tq, wacc), F32))
            total = total + jnp.sum(acc, axis=1, keepdims=True)
        return total

    def bisect_body(_, c):
        lo, hi = c
        mid = (lo >> 1) + (hi >> 1) + (((lo & 1) + (hi & 1) + 1) >> 1)
        ge = count_ge(mid) >= float(k_sel)
        return jnp.where(ge, mid, lo), jnp.where(ge, hi, mid - 1)

    n_iter = jnp.where(n_valid > k_sel, 32, 0)
    thr, _ = lax.fori_loop(0, n_iter, bisect_body,
                           (jnp.full((tq, 1), INT_MIN + 1, I32), jnp.full((tq, 1), INT_MAX, I32)))

    def selected(off, col, tk):
        return keys_ref[:, pl.ds(col, tk)] >= thr

    att_sources = [(k_ref, v_ref, base, tk, nt, True) for (k_ref, v_ref, _, base, tk, nt, _) in sources]
    o_ref[...] = _pair_heads(_attend(q_ref, att_sources, s_scr, tq, H_B, selected))


def _dsa_attn(q, qi, wi, kn, vn, kin, past, tq, causal, first_key, k_sel):
    b, _, l, _ = q.shape
    ln = kn.shape[2]
    tkn = KEY_TILE if ln % KEY_TILE == 0 else ln
    qspec = pl.BlockSpec((None, H_B, tq, HEAD_W), lambda bi, i: (bi, 0, i, 0))
    whole = lambda n: pl.BlockSpec((None, H_B, n, HEAD_W), lambda bi, i: (bi, 0, 0, 0))
    whole_ki = lambda n: pl.BlockSpec((None, n, D_I), lambda bi, i: (bi, 0, 0))
    args = [q, qi, wi]
    specs = [qspec, pl.BlockSpec((None, H_I, tq, D_I), lambda bi, i: (bi, 0, i, 0)),
             pl.BlockSpec((None, tq, LANE), lambda bi, i: (bi, i, 0))]
    p_len = 0
    if past is not None:
        p_len = past[0].shape[2]
        args += list(past)
        specs += [whole(p_len), whole(p_len), whole_ki(p_len)]
    args += [kn, vn, kin]
    specs += [whole(ln), whole(ln), whole_ki(ln)]
    kern = functools.partial(_dsa_attn_kernel, tq=tq, tkn=tkn, ln=ln, causal=causal, first_key=first_key,
                             p_len=p_len, tkp=KEY_TILE, k_sel=k_sel)
    return pl.pallas_call(
        kern, grid=(b, l // tq), in_specs=specs,
        out_specs=pl.BlockSpec((None, tq, H_B * DH_B), lambda bi, i: (bi, i, 0)),
        out_shape=jax.ShapeDtypeStruct((b, l, H_B * DH_B), F32),
        scratch_shapes=[pltpu.VMEM((tq, p_len + _round_up(ln, TILE_UNROLL * tkn)), I32),
                        pltpu.VMEM((H_B, tq, p_len + _round_up(ln, TILE_UNROLL * tkn)), F32),
                        pltpu.VMEM((H_I, tq, LANE), F32)],
        compiler_params=_cparams("parallel", "arbitrary"), name="dsa_attn")(*args)


def _softplus(x):
    return jnp.maximum(x, 0.0) + jnp.log(1.0 + jnp.exp(-jnp.abs(x)))


def _rwkv_prep_kernel(zc_ref, zp_ref, sh_ref, mu_ref, w0_ref, a0_ref, kkw_ref, ka_ref,
                      wup_ref, aup_ref, gup_ref, bd_ref,
                      r_ref, k_ref, v_ref, ld_ref, kk_ref, b_ref, g_ref, *, tl, lo, hi):
    i = pl.program_id(1)
    z = zc_ref[...]
    rowi = lax.broadcasted_iota(I32, (tl, 1), 0)
    prev_last = jnp.where(i == 0, sh_ref[...], zp_ref[7:8, :])
    zprev = jnp.where(rowi == 0, prev_last, pltpu.roll(z, 1, 0))
    zs = z + (zprev - z) * mu_ref[...]
    r, k, v, lora = zs[:, :W_C], zs[:, W_C:2 * W_C], zs[:, 2 * W_C:3 * W_C], zs[:, 3 * W_C:]
    lane = lax.broadcasted_iota(I32, (1, LANE), 1)
    u = jnp.where(lane < W_LORA, jnp.tanh(lora),
                  jnp.where(lane < W_LORA + A_LORA, lora, jax.nn.sigmoid(lora)))
    w = -_softplus(-(w0_ref[...] + _mm(u, wup_ref[...]))) - 0.5
    ld = -jnp.exp(w)
    a = jax.nn.sigmoid(a0_ref[...] + _mm(u, aup_ref[...]))
    g = _mm(u, gup_ref[...])
    kmod = k * (1.0 + (a - 1.0) * ka_ref[...])
    kk = k * kkw_ref[...]
    sq = kk * kk
    bd = bd_ref[...]
    if bd.dtype == F32:
        ss = _mm(sq, bd)
    else:
        sq_hi = sq.astype(BF16)
        ss = _mm(sq_hi, bd) + _mm(sq - sq_hi.astype(F32), bd)
    kkn = kk * jnp.minimum(lax.rsqrt(ss), 1e12)
    grow = i * tl + rowi
    valid = ((grow >= lo) & (grow < hi)).astype(F32)
    r_ref[...] = r
    k_ref[...] = kmod * valid
    v_ref[...] = v * valid
    ld_ref[...] = ld * valid
    kk_ref[...] = kkn * valid
    b_ref[...] = kkn * a * valid
    g_ref[...] = g


def _rwkv_prep(zc3, shift, p, tl, lo, hi):
    b, l, _ = zc3.shape
    rows = lambda w: pl.BlockSpec((None, tl, w), lambda bi, i: (bi, i, 0))
    vec = _full((1, W_C))
    mat = _full((LANE, W_C))
    kern = functools.partial(_rwkv_prep_kernel, tl=tl, lo=lo, hi=hi)
    return pl.pallas_call(
        kern, grid=(b, l // tl),
        in_specs=[rows(C_W),
                  pl.BlockSpec((None, 8, C_W), lambda bi, i: (bi, jnp.maximum(i * (tl // 8) - 1, 0), 0)),
                  pl.BlockSpec((None, 1, C_W), lambda bi, i: (bi, 0, 0)),
                  _full((1, C_W)), vec, vec, vec, vec, mat, mat, mat, _full((W_C, W_C))],
        out_specs=[rows(W_C)] * 7,
        out_shape=[jax.ShapeDtypeStruct((b, l, W_C), F32)] * 7,
        compiler_params=_cparams("parallel", "parallel"), name="rwkv_prep")(
            zc3, zc3, shift, p['mu'], p['w0'], p['a0'], p['k_k'], p['k_a'], p['wup'], p['aup'], p['gup'], p['bd'])


def _rwkv_scan_kernel(r_ref, k_ref, v_ref, ld_ref, kk_ref, b_ref, g_ref, s0_ref, lng_ref, lnb_ref, rk_ref,
                      o_ref, sout_ref, s_scr, *, c, n_chunks, md):
    ci = pl.program_id(1)

    @pl.when(ci == 0)
    def _():
        s_scr[...] = s0_ref[...]

    ld = ld_ref[...]
    rowi = lax.broadcasted_iota(I32, (c, 1), 0)
    cum = ld
    step = 1
    while step < c:
        cum = cum + jnp.where(rowi >= step, pltpu.roll(cum, step, 0), 0.0)
        step *= 2
    cl = cum[c - 1:c, :]
    e_in, e_ex, e_neg, e_end = jnp.exp(cum), jnp.exp(cum - ld), jnp.exp(-cum), jnp.exp(cl - cum)
    d_end = jnp.exp(cl)
    r, k, v, kk, bb = r_ref[...], k_ref[...], v_ref[...], kk_ref[...], b_ref[...]
    rt, kap, kt, bt = r * e_in, kk * e_ex, k * e_neg, bb * e_neg
    kte, bte = k * e_end, bb * e_end
    bonus_rk = r * k * rk_ref[...]
    ri = lax.broadcasted_iota(I32, (c, c), 0)
    cj = lax.broadcasted_iota(I32, (c, c), 1)
    strict, incl = ri > cj, ri >= cj
    eye = jnp.where(ri == cj, 1.0, 0.0)
    n_sq = c.bit_length() - 2
    heads = range(H_C)
    sls = [slice(h * N_C, (h + 1) * N_C) for h in heads]
    s_old = [s_scr[h] for h in heads]
    s_mm = [s.astype(md) for s in s_old]
    left = [jnp.concatenate([kap[:, sl], rt[:, sl]], axis=0).astype(md) for sl in sls]
    right = [jnp.concatenate([bt[:, sl], kt[:, sl]], axis=0).astype(md) for sl in sls]
    gram = [_mm_t(left[h], right[h]) for h in heads]
    a_kb = [jnp.where(strict, g[:c, :c], 0.0) for g in gram]
    a_kk = [jnp.where(strict, g[:c, c:], 0.0) for g in gram]
    a_rb = [jnp.where(incl, g[c:, :c], 0.0) for g in gram]
    a_rk = [jnp.where(incl, g[c:, c:], 0.0) for g in gram]
    vh = [v[:, sl] for sl in sls]
    vb = [x.astype(md) for x in vh]
    x0 = [_mm_t(left[h][:c], s_mm[h]) + _mm(a_kk[h], vb[h]) for h in heads]
    y0 = [_mm_t(left[h][c:], s_mm[h]) + _mm(a_rk[h], vb[h]) for h in heads]
    tinv = [eye - n for n in a_kb]
    npow = a_kb
    for _ in range(n_sq):
        nb = [n.astype(md) for n in npow]
        npow = [_mm(n, n) for n in nb]
        tinv = [tinv[h] + _mm(tinv[h], npow[h].astype(md)) for h in heads]
    w = [_mm(tinv[h], x0[h].astype(md)) for h in heads]
    wb = [x.astype(md) for x in w]
    y = [y0[h] - _mm(a_rb[h], wb[h]) for h in heads]
    for h in heads:
        sl = sls[h]
        s_scr[h] = (s_old[h] * d_end[:, sl] + _mm(vh[h].T, kte[:, sl].astype(md))
                    - _mm(w[h].T, bte[:, sl].astype(md)))
    for h in heads:
        sl = sls[h]
        ym = jnp.mean(y[h], axis=-1, keepdims=True)
        yc = y[h] - ym
        yn = yc * lax.rsqrt(jnp.mean(yc * yc, axis=-1, keepdims=True) + GN_EPS) * lng_ref[:, sl] + lnb_ref[:, sl]
        bonus = jnp.sum(bonus_rk[:, sl], axis=-1, keepdims=True) * vh[h]
        o_ref[:, sl] = (yn + bonus) * g_ref[:, sl]

    @pl.when(ci == n_chunks - 1)
    def _():
        sout_ref[...] = s_scr[...]


def _rwkv_scan(pre, s0, p, c, mm_dtype):
    r = pre[0]
    b, l, _ = r.shape
    n_chunks = l // c
    rows = pl.BlockSpec((None, c, W_C), lambda bi, i: (bi, i, 0))
    state = pl.BlockSpec((None, H_C, N_C, N_C), lambda bi, i: (bi, 0, 0, 0))
    vec = _full((1, W_C))
    kern = functools.partial(_rwkv_scan_kernel, c=c, n_chunks=n_chunks, md=mm_dtype)
    return pl.pallas_call(
        kern, grid=(b, n_chunks),
        in_specs=[rows] * 7 + [state, vec, vec, vec],
        out_specs=[rows, state],
        out_shape=[jax.ShapeDtypeStruct((b, l, W_C), F32), jax.ShapeDtypeStruct((b, H_C, N_C, N_C), F32)],
        scratch_shapes=[pltpu.VMEM((H_C, N_C, N_C), F32)],
        compiler_params=_cparams("parallel", "arbitrary"), name="rwkv_scan")(
            *pre, s0, p['ln_g'], p['ln_b'], p['r_k'])


def _layernorm(y, g, b):
    mu = jnp.mean(y, axis=-1, keepdims=True)
    yc = y - mu
    return yc * lax.rsqrt(jnp.mean(yc * yc, axis=-1, keepdims=True) + LN_EPS) * g + b


def _row_valid(i, tm, tiles_per_batch, lo, hi):
    rowb = (i % tiles_per_batch) * tm + lax.broadcasted_iota(I32, (tm, 1), 0)
    return (rowb >= lo) & (rowb < hi)


def _outproj_kernel(oa_ref, ob_ref, oc_ref, x_ref, wa_ref, wb_ref, wc_ref, g_ref, b_ref, y_ref,
                    *, tm, tiles_per_batch, lo, hi, masked):
    mix = _mm(oa_ref[...], wa_ref[...]) + _mm(ob_ref[...], wb_ref[...]) + _mm(oc_ref[...], wc_ref[...])
    y = _layernorm(ALPHA * x_ref[...] + mix, g_ref[...], b_ref[...])
    if masked:
        y = jnp.where(_row_valid(pl.program_id(0), tm, tiles_per_batch, lo, hi), y, 0.0)
    y_ref[...] = y


def _outproj(oa, ob, oc, x2d, p, tm, rows_per_batch, lo, hi, masked):
    t = x2d.shape[0]
    row = lambda w: pl.BlockSpec((tm, w), lambda i: (i, 0))
    kern = functools.partial(_outproj_kernel, tm=tm, tiles_per_batch=rows_per_batch // tm, lo=lo, hi=hi,
                             masked=masked)
    return pl.pallas_call(
        kern, grid=(t // tm,),
        in_specs=[row(oa.shape[1]), row(ob.shape[1]), row(oc.shape[1]), row(D_MODEL),
                  _full(p['wo_a'].shape), _full(p['wo_b'].shape), _full(p['wo_c'].shape),
                  _full((1, D_MODEL)), _full((1, D_MODEL))],
        out_specs=row(D_MODEL), out_shape=jax.ShapeDtypeStruct((t, D_MODEL), F32),
        compiler_params=_cparams("parallel"), name="outproj_ln")(
            oa, ob, oc, x2d, p['wo_a'], p['wo_b'], p['wo_c'], p['ln1_g'], p['ln1_b'])


def _router_kernel(x_ref, whi_ref, wlo_ref, br_ref, dg_ref):
    x = x_ref[...]
    xh = x.astype(BF16)
    xl = (x - xh.astype(F32)).astype(BF16)
    logits = _mm(xh, whi_ref[...]) + _mm(xl, whi_ref[...]) + _mm(xh, wlo_ref[...]) + br_ref[...]
    lane = lax.broadcasted_iota(I32, (1, LANE), 1).astype(F32)
    big = float(LANE)
    gm = (lane >= N_EXPERTS) & (lane < N_EXPERTS + N_GROUPS)
    gl = jnp.where(gm, logits, -jnp.inf)
    ge = jnp.exp(gl - jnp.max(gl, axis=1, keepdims=True))
    gp = ge / jnp.sum(ge, axis=1, keepdims=True)
    gval = jnp.max(gp, axis=1, keepdims=True)
    gidx = jnp.min(jnp.where(gm & (gp == gval), lane, big), axis=1, keepdims=True) - N_EXPERTS
    elo = gidx * EXP_PER_GROUP
    em = (lane >= elo) & (lane < elo + EXP_PER_GROUP)
    el = jnp.where(em, logits, -jnp.inf)
    ee = jnp.exp(el - jnp.max(el, axis=1, keepdims=True))
    ep = jnp.where(em, ee / jnp.sum(ee, axis=1, keepdims=True), -1.0)
    p1 = jnp.max(ep, axis=1, keepdims=True)
    i1 = jnp.min(jnp.where(ep == p1, lane, big), axis=1, keepdims=True)
    ep2 = jnp.where(lane == i1, -1.0, ep)
    p2 = jnp.max(ep2, axis=1, keepdims=True)
    i2 = jnp.min(jnp.where((ep2 == p2) & (lane != i1), lane, big), axis=1, keepdims=True)
    den = p1 + p2
    dg_ref[...] = jnp.where(lane == i1, gval * p1 / den, 0.0) + jnp.where(lane == i2, gval * p2 / den, 0.0)


def _router(x2d, p, tm):
    t = x2d.shape[0]
    row = lambda w: pl.BlockSpec((tm, w), lambda i: (i, 0))
    return pl.pallas_call(
        _router_kernel, grid=(t // tm,),
        in_specs=[row(D_MODEL), _full((D_MODEL, LANE)), _full((D_MODEL, LANE)), _full((1, LANE))],
        out_specs=row(LANE), out_shape=jax.ShapeDtypeStruct((t, LANE), F32),
        compiler_params=_cparams("parallel"), name="router")(x2d, p['wr_hi'], p['wr_lo'], p['br'])


def _moe_kernel(x_ref, dg_ref, wgu_ref, wd_ref, g_ref, b_ref, y_ref, acc_ref, xb_ref,
                *, tm, tiles_per_batch, lo, hi, masked, fused_down):
    e = pl.program_id(1)

    @pl.when(e == 0)
    def _():
        xb_ref[...] = x_ref[...].astype(xb_ref.dtype)
        if not fused_down:
            acc_ref[...] = jnp.zeros_like(acc_ref)

    hcat = _mm(xb_ref[...], wgu_ref[...])
    hg, hu = hcat[:, :D_EXPERT], hcat[:, D_EXPERT:]
    lane = lax.broadcasted_iota(I32, (1, LANE), 1)
    gate = jnp.sum(jnp.where(lane == e, dg_ref[...], 0.0), axis=1, keepdims=True)
    hidden = hg * jax.nn.sigmoid(hg) * hu * gate
    if fused_down:
        acc_ref[:, pl.ds(pl.multiple_of(e * D_EXPERT, D_EXPERT), D_EXPERT)] = hidden.astype(acc_ref.dtype)
    else:
        acc_ref[...] += _mm(hidden, wd_ref[...])

    @pl.when(e == N_EXPERTS - 1)
    def _():
        ffn = _mm(acc_ref[...], wd_ref[...]) if fused_down else acc_ref[...]
        y = _layernorm(ALPHA * x_ref[...] + ffn, g_ref[...], b_ref[...])
        if masked:
            y = jnp.where(_row_valid(pl.program_id(0), tm, tiles_per_batch, lo, hi), y, 0.0)
        y_ref[...] = y


def _moe(x2d, dg, p, tm, rows_per_batch, lo, hi, masked):
    t = x2d.shape[0]
    row = lambda w: pl.BlockSpec((tm, w), lambda i, e: (i, 0))
    md = p['wgu'].dtype
    fused_down = md == BF16
    kern = functools.partial(_moe_kernel, tm=tm, tiles_per_batch=rows_per_batch // tm, lo=lo, hi=hi,
                             masked=masked, fused_down=fused_down)
    if fused_down:
        wd = p['wd'].reshape(N_EXPERTS * D_EXPERT, D_MODEL)
        wd_spec = pl.BlockSpec(wd.shape, lambda i, e: (0, 0), pipeline_mode=pl.Buffered(1))
        acc = pltpu.VMEM((tm, N_EXPERTS * D_EXPERT), md)
    else:
        wd = p['wd']
        wd_spec = pl.BlockSpec((None, D_EXPERT, D_MODEL), lambda i, e: (e, 0, 0))
        acc = pltpu.VMEM((tm, D_MODEL), F32)
    return pl.pallas_call(
        kern, grid=(t // tm, N_EXPERTS),
        in_specs=[row(D_MODEL), row(LANE),
                  pl.BlockSpec((None, D_MODEL, 2 * D_EXPERT), lambda i, e: (e, 0, 0)), wd_spec,
                  pl.BlockSpec((1, D_MODEL), lambda i, e: (0, 0)), pl.BlockSpec((1, D_MODEL), lambda i, e: (0, 0))],
        out_specs=row(D_MODEL), out_shape=jax.ShapeDtypeStruct((t, D_MODEL), F32),
        scratch_shapes=[acc, pltpu.VMEM((tm, D_MODEL), md)],
        compiler_params=_cparams("parallel", "arbitrary"), name="moe_ln")(
            x2d, dg, p['wgu'], wd, p['ln2_g'], p['ln2_b'])


def _rope_table(pos, pattern):
    posf = pos.astype(F32)[:, None]
    n = pos.shape[0]
    cos, s_lo, s_hi = [], [], []
    for kind, w in pattern:
        if kind == 'rope':
            half = w // 2
            inv = ROPE_THETA ** (-jnp.arange(half, dtype=F32) / half)
            ang = posf * inv[None, :]
            c, s, z = jnp.cos(ang), jnp.sin(ang), jnp.zeros((n, half), F32)
            cos += [c, c]
            s_lo += [-s, z]
            s_hi += [z, s]
        else:
            fill = jnp.full((n, w), 1.0 if kind == 'one' else 0.0, F32)
            z = jnp.zeros((n, w), F32)
            cos.append(fill)
            s_lo.append(z)
            s_hi.append(z)
    return jnp.stack([jnp.concatenate(t, axis=1) for t in (cos, s_lo, s_hi)])


def _tables(pos):
    return {
        'mla_q': _rope_table(pos, [('one', NOPE_A), ('rope', ROPE_A), ('zero', HEAD_W - NOPE_A - ROPE_A)]),
        'mla_kr': _rope_table(pos, [('rope', ROPE_A), ('zero', LANE - ROPE_A)]),
        'dsa_qk': _rope_table(pos, [('rope', DH_B), ('zero', HEAD_W - DH_B)]),
        'idx_q': _rope_table(pos, [('rope', D_I)] * (LANE // D_I)),
        'idx_k': _rope_table(pos, [('rope', D_I), ('one', H_I), ('zero', LANE - D_I - H_I)]),
    }


def _pad_cols(w, width):
    return jnp.pad(w, [(0, 0)] * (w.ndim - 1) + [(0, width - w.shape[-1])])


def _head_pad(w, n_heads, width):
    d = w.shape[-1] // n_heads
    w = w.reshape(w.shape[:-1] + (n_heads, d))
    return _pad_cols(w, width).reshape(w.shape[:-2] + (n_heads * width,))


def _prepare_weights(w, md):
    w_in = w['w_in']
    a, bseg, cseg = w_in[..., :A_COLS], w_in[..., A_COLS:A_COLS + B_COLS], w_in[..., A_COLS + B_COLS:]
    hd = H_B * DH_B
    wb = jnp.concatenate([
        _head_pad(bseg[..., 0:hd], H_B, HEAD_W), _head_pad(bseg[..., hd:2 * hd], H_B, HEAD_W),
        _head_pad(bseg[..., 2 * hd:3 * hd], H_B, HEAD_W), bseg[..., 3 * hd:3 * hd + H_I * D_I],
        _pad_cols(bseg[..., 3 * hd + H_I * D_I:], LANE)], axis=-1)
    w_uq = w['mla_w_uq'].reshape(DEPTH, Q_LORA, H_A, NOPE_A + ROPE_A)
    w_ukv = w['mla_w_ukv'].reshape(DEPTH, KV_LORA, H_A, NOPE_A + V_A)
    zeros_lora = lambda n: jnp.zeros((DEPTH, n, W_C), F32)
    head_of = jnp.arange(W_C) // N_C
    wr = jnp.concatenate([w['router_expert'], w['router_group']], axis=-1)
    wr = _pad_cols(wr, LANE)
    wr_hi = wr.astype(BF16)
    row = lambda v: v.reshape(DEPTH, 1, -1)
    return {
        'wa': _pad_cols(a, A_W).astype(md), 'wb': wb.astype(md), 'wc': cseg.astype(md),
        'qg': row(w['mla_q_norm']), 'kg': row(w['mla_kv_norm']),
        'wuq': _pad_cols(w_uq, HEAD_W).reshape(DEPTH, Q_LORA, H_A * HEAD_W).astype(md),
        'wk': _pad_cols(w_ukv[..., :NOPE_A], HEAD_W).reshape(DEPTH, KV_LORA, H_A * HEAD_W).astype(md),
        'wv': _pad_cols(w_ukv[..., NOPE_A:], HEAD_W).reshape(DEPTH, KV_LORA, H_A * HEAD_W).astype(md),
        'mu': row(w['rwkv_mu']), 'w0': row(w['rwkv_w0']), 'a0': row(w['rwkv_a0']),
        'k_k': row(w['rwkv_k_k']), 'k_a': row(w['rwkv_k_a']), 'r_k': row(w['rwkv_r_k']),
        'ln_g': row(w['rwkv_ln_g']), 'ln_b': row(w['rwkv_ln_b']),
        'wup': jnp.concatenate([w['rwkv_w_up'], zeros_lora(LANE - W_LORA)], axis=1).astype(md),
        'aup': jnp.concatenate([zeros_lora(W_LORA), w['rwkv_a_up'], zeros_lora(G_LORA)], axis=1).astype(md),
        'gup': jnp.concatenate([zeros_lora(W_LORA + A_LORA), w['rwkv_g_up']], axis=1).astype(md),
        'bd': jnp.broadcast_to((head_of[:, None] == head_of[None, :]).astype(md), (DEPTH, W_C, W_C)),
        'wo_a': w['w_out'][:, :H_A * V_A].astype(md),
        'wo_b': w['w_out'][:, H_A * V_A:H_A * V_A + hd].astype(md),
        'wo_c': w['w_out'][:, H_A * V_A + hd:].astype(md),
        'ln1_g': row(w['ln1_g']), 'ln1_b': row(w['ln1_b']), 'ln2_g': row(w['ln2_g']), 'ln2_b': row(w['ln2_b']),
        'wr_hi': wr_hi, 'wr_lo': (wr - wr_hi.astype(F32)).astype(BF16),
        'br': _pad_cols(jnp.concatenate([w['router_expert_b'], w['router_group_b']], axis=-1), LANE).reshape(DEPTH, 1, LANE),
        'wgu': jnp.concatenate([w['exp_w_gate'], w['exp_w_up']], axis=-1).astype(md),
        'wd': w['exp_w_down'].astype(md),
    }


def _largest_tile(n, limit):
    best = 8
    for t in range(8, min(n, limit) + 1, 8):
        if n % t == 0:
            best = t
    return best


def _trunk(x, pos, past, wp, *, lo, hi, causal, k_sel):
    b, l, _ = x.shape
    t = b * l
    tabs = _tables(pos)
    tl = min(256, l)
    tq_mla = min(4 * CHUNK, l)
    tq_dsa = min(2 * CHUNK, l)
    c_rwkv = min(CHUNK, l)
    masked = not (lo == 0 and hi == l)
    md = wp['wa'].dtype
    tm_cap = 512 if md == BF16 else 128
    tm = _largest_tile(l, tm_cap) if masked else _largest_tile(t, tm_cap)
    tm_moe = _largest_tile(l, 1100) if masked else _largest_tile(t, 1100)
    first_key = lo if causal else 0
    new = {k: [] for k in ('ckv', 'krope', 'dsa_k', 'dsa_v', 'dsa_kidx', 'rwkv', 'shift')}
    x2d = x.reshape(t, D_MODEL)
    for layer in range(DEPTH):
        p = {k: v[layer] for k, v in wp.items()}
        za, zb, zc = _proj_in(x2d, p['wa'], p['wb'], p['wc'], tm)
        za3, zb3, zc3 = za.reshape(b, l, A_W), zb.reshape(b, l, B_W), zc.reshape(b, l, C_W)

        q, kc, v, ckv_new, kr_new = _mla_prep(za3, tabs['mla_q'], tabs['mla_kr'], p['qg'], p['kg'],
                                              p['wuq'], p['wk'], p['wv'], tl)
        mla_past = None
        if past is not None:
            mla_past = _mla_past(past['ckv'][layer], _pad_cols(past['krope'][layer], LANE), p['wk'], p['wv'], 512)
        oa = _mla_attn(q, kc, v, mla_past, tq_mla, causal, first_key)

        qd, kd, vd, qi, ki, wi, kst, vst, kist = _dsa_prep(zb3, tabs['dsa_qk'], tabs['idx_q'], tabs['idx_k'], tl, md)
        dsa_past = None
        if past is not None:
            to_heads = lambda c: _pad_cols(jnp.swapaxes(c.astype(md), 1, 2), HEAD_W)
            ones_col = (jnp.arange(HEAD_W) == DH_B).astype(md)
            dsa_past = (to_heads(past['dsa_k'][layer]), to_heads(past['dsa_v'][layer]) + ones_col,
                        past['dsa_kidx'][layer].astype(md))
        ob = _dsa_attn(qd, qi, wi, kd, vd, ki, dsa_past, tq_dsa, causal, first_key, k_sel)

        if past is not None:
            shift, s0 = past['shift'][layer][:, None, :], past['rwkv'][layer]
        else:
            shift, s0 = jnp.zeros((b, 1, C_W), F32), jnp.zeros((b, H_C, N_C, N_C), F32)
        pre = _rwkv_prep(zc3, shift, p, tl, lo, hi)
        oc, s_last = _rwkv_scan(pre, s0, p, c_rwkv, md)

        x1 = _outproj(oa.reshape(t, -1), ob.reshape(t, -1), oc.reshape(t, -1), x2d, p, tm, l, lo, hi, masked)
        dg = _router(x1, p, tm)
        x2d = _moe(x1, dg, p, tm_moe, l, lo, hi, masked)

        n = hi - lo
        new['ckv'].append(ckv_new[:, lo:hi])
        new['krope'].append(kr_new[:, lo:hi])
        new['dsa_k'].append(kst[:, lo:hi].reshape(b, n, H_B, DH_B))
        new['dsa_v'].append(vst[:, lo:hi].reshape(b, n, H_B, DH_B))
        new['dsa_kidx'].append(kist[:, lo:hi])
        new['rwkv'].append(s_last)
        new['shift'].append(zc3[:, hi - 1, :])
    return x2d.reshape(b, l, D_MODEL), {k: jnp.stack(v) for k, v in new.items()}


def kernel(x_prompt, x_sample, cache_mla_ckv, cache_mla_krope, cache_dsa_k, cache_dsa_v, cache_dsa_kidx, state_rwkv, state_rwkv_shift, meta_tokens, w_in, mla_q_norm, mla_kv_norm, mla_w_uq, mla_w_ukv, rwkv_mu, rwkv_w0, rwkv_w_up, rwkv_a0, rwkv_a_up, rwkv_g_up, rwkv_k_k, rwkv_k_a, rwkv_r_k, rwkv_ln_g, rwkv_ln_b, w_out, ln1_g, ln1_b, ln2_g, ln2_b, router_group, router_group_b, router_expert, router_expert_b, exp_w_gate, exp_w_up, exp_w_down):
    weights = {
        'w_in': w_in, 'mla_q_norm': mla_q_norm, 'mla_kv_norm': mla_kv_norm, 'mla_w_uq': mla_w_uq,
        'mla_w_ukv': mla_w_ukv, 'rwkv_mu': rwkv_mu, 'rwkv_w0': rwkv_w0, 'rwkv_w_up': rwkv_w_up,
        'rwkv_a0': rwkv_a0, 'rwkv_a_up': rwkv_a_up, 'rwkv_g_up': rwkv_g_up, 'rwkv_k_k': rwkv_k_k,
        'rwkv_k_a': rwkv_k_a, 'rwkv_r_k': rwkv_r_k, 'rwkv_ln_g': rwkv_ln_g, 'rwkv_ln_b': rwkv_ln_b,
        'w_out': w_out, 'ln1_g': ln1_g, 'ln1_b': ln1_b, 'ln2_g': ln2_g, 'ln2_b': ln2_b,
        'router_group': router_group, 'router_group_b': router_group_b, 'router_expert': router_expert,
        'router_expert_b': router_expert_b, 'exp_w_gate': exp_w_gate, 'exp_w_up': exp_w_up,
        'exp_w_down': exp_w_down,
    }
    wp = _prepare_weights(weights, BF16)
    wp_sample = _prepare_weights(weights, F32)

    bp, seq, _ = x_prompt.shape
    lo, hi = FRONT_PAD, FRONT_PAD + N_META + seq
    lp = -(-hi // KEY_TILE) * KEY_TILE
    meta = jnp.broadcast_to(meta_tokens.astype(F32)[None], (bp, N_META, D_MODEL))
    xp = jnp.concatenate([jnp.zeros((bp, lo, D_MODEL), F32), meta, x_prompt,
                          jnp.zeros((bp, lp - hi, D_MODEL), F32)], axis=1)
    pos_p = jnp.arange(lp, dtype=jnp.int32) - lo
    yp, new_p = _trunk(xp, pos_p, None, wp, lo=lo, hi=hi, causal=True, k_sel=min(TOPK_MAX, seq // 4))
    y_prompt = yp[:, lo + N_META:hi]

    n_past, n_new = cache_mla_ckv.shape[2], x_sample.shape[1]
    assert n_past % (TILE_UNROLL * KEY_TILE) == 0 and n_new <= CHUNK and n_new % 8 == 0, (n_past, n_new)
    pos_s = n_past + jnp.arange(n_new, dtype=jnp.int32)
    past_s = {'ckv': cache_mla_ckv, 'krope': cache_mla_krope, 'dsa_k': cache_dsa_k, 'dsa_v': cache_dsa_v,
              'dsa_kidx': cache_dsa_kidx, 'rwkv': state_rwkv, 'shift': state_rwkv_shift}
    y_sample, new_s = _trunk(x_sample, pos_s, past_s, wp_sample, lo=0, hi=n_new, causal=False,
                             k_sel=min(TOPK_MAX, (n_past + n_new) // 4))

    keys = ('ckv', 'krope', 'dsa_k', 'dsa_v', 'dsa_kidx', 'rwkv', 'shift')
    return (y_prompt, y_sample) + tuple(new_p[k] for k in keys) + tuple(new_s[k] for k in keys)
```

```python
import functools

import jax
import jax.numpy as jnp
from jax import lax
from jax.experimental import pallas as pl
from jax.experimental.pallas import tpu as pltpu

F32 = jnp.float32
BF16 = jnp.bfloat16
I32 = jnp.int32

D_MODEL = 1024
DEPTH = 4
CHUNK = 64
N_META = 16
ROPE_THETA = 10000.0
NEG_INF = -1e30
LN_EPS = 1e-5
RMS_EPS = 1e-6
GN_EPS = 64e-5
ALPHA = (2 * DEPTH) ** 0.25

H_A, Q_LORA, KV_LORA, NOPE_A, ROPE_A, V_A = 4, 256, 128, 64, 32, 64
H_B, DH_B, H_I, D_I, TOPK_MAX = 4, 64, 8, 32, 256
H_C, N_C, W_C, W_LORA, A_LORA, G_LORA = 8, 64, 512, 32, 32, 64
N_GROUPS, EXP_PER_GROUP, N_EXPERTS, D_EXPERT = 4, 4, 16, 256

A_COLS = Q_LORA + KV_LORA + ROPE_A
B_COLS = 3 * H_B * DH_B + H_I * D_I + D_I + H_I
C_COLS = 3 * W_C + W_LORA + A_LORA + G_LORA

MLA_SCALE = (NOPE_A + ROPE_A) ** -0.5
DSA_SCALE = DH_B ** -0.5
IDX_SCALE = (H_I ** -0.5) * (D_I ** -0.5)

LANE = 128
HEAD_W = 128
A_W = Q_LORA + KV_LORA + LANE
B_Q, B_K, B_V = 0, H_B * HEAD_W, 2 * H_B * HEAD_W
B_QI = 3 * H_B * HEAD_W
B_KI = B_QI + H_I * D_I
B_W = B_KI + LANE
C_W = C_COLS
FRONT_PAD = CHUNK - N_META
KEY_TILE = 256
TILE_UNROLL = 2
INT_MIN = -2147483648
INT_MAX = 2147483647
VMEM_LIMIT = 56 * 1024 * 1024


def _cparams(*sem):
    return pltpu.CompilerParams(dimension_semantics=sem, vmem_limit_bytes=VMEM_LIMIT)


def _full(shape):
    nd = len(shape)
    return pl.BlockSpec(shape, lambda *_: (0,) * nd)


def _round_up(n, m):
    return -(-n // m) * m


def _tile_lanes(t, reps):
    return t if reps == 1 else jnp.concatenate([t] * reps, axis=1)


def _rope(x, tab_ref, half):
    n = x.shape[1]
    reps = n // LANE
    cos = _tile_lanes(tab_ref[0], reps)
    sin_lo = _tile_lanes(tab_ref[1], reps)
    sin_hi = _tile_lanes(tab_ref[2], reps)
    return x * cos + pltpu.roll(x, n - half, 1) * sin_lo + pltpu.roll(x, half, 1) * sin_hi


def _mm(a, b):
    if b.dtype == F32:
        return jnp.dot(a.astype(F32), b, preferred_element_type=F32, precision=lax.Precision.HIGHEST)
    return jnp.dot(a.astype(BF16), b, preferred_element_type=F32)


def _mm_t(a, b):
    prec = lax.Precision.HIGHEST if b.dtype == F32 else None
    return lax.dot_general(a.astype(b.dtype), b, (((1,), (1,)), ((), ())), preferred_element_type=F32,
                           precision=prec)


def _pair_heads(parts):
    lo = parts[0] + pltpu.roll(parts[1], 64, 1)
    hi = parts[2] + pltpu.roll(parts[3], 64, 1)
    return jnp.concatenate([lo, hi], axis=1)


def _proj_in_kernel(x_ref, wa_ref, wb_ref, wc_ref, za_ref, zb_ref, zc_ref):
    xb = x_ref[...].astype(wa_ref.dtype)
    za_ref[...] = _mm(xb, wa_ref[...])
    zb_ref[...] = _mm(xb, wb_ref[...])
    zc_ref[...] = _mm(xb, wc_ref[...])


def _proj_in(x2d, wa, wb, wc, tm):
    t = x2d.shape[0]
    row = lambda w: pl.BlockSpec((tm, w), lambda i: (i, 0))
    return pl.pallas_call(
        _proj_in_kernel, grid=(t // tm,),
        in_specs=[row(D_MODEL), _full(wa.shape), _full(wb.shape), _full(wc.shape)],
        out_specs=[row(A_W), row(B_W), row(C_W)],
        out_shape=[jax.ShapeDtypeStruct((t, w), F32) for w in (A_W, B_W, C_W)],
        compiler_params=_cparams("parallel"), name="proj_in")(x2d, wa, wb, wc)


def _rms(x, g):
    return x * lax.rsqrt(jnp.mean(x * x, axis=-1, keepdims=True) + RMS_EPS) * g


def _mla_prep_kernel(za_ref, tq_ref, tk_ref, qg_ref, kg_ref, wuq_ref, wk_ref, wv_ref,
                     q_ref, kc_ref, v_ref, ckv_ref, kr_ref):
    za = za_ref[...]
    qn = _rms(za[:, :Q_LORA], qg_ref[...])
    q = _rope(_mm(qn, wuq_ref[...]), tq_ref, ROPE_A // 2) * MLA_SCALE
    ckvn = _rms(za[:, Q_LORA:Q_LORA + KV_LORA], kg_ref[...])
    ckv_ref[...] = ckvn
    kr = _rope(za[:, Q_LORA + KV_LORA:], tk_ref, ROPE_A // 2)
    kr_ref[...] = kr[:, :ROPE_A]
    kc = _mm(ckvn, wk_ref[...]) + _tile_lanes(pltpu.roll(kr, NOPE_A, 1), H_A)
    v = _mm(ckvn, wv_ref[...]) + _ones_lane(H_A * HEAD_W)
    for h in range(H_A):
        sl = slice(h * HEAD_W, (h + 1) * HEAD_W)
        q_ref[h] = q[:, sl].astype(q_ref.dtype)
        kc_ref[h] = kc[:, sl].astype(kc_ref.dtype)
        v_ref[h] = v[:, sl].astype(v_ref.dtype)


def _mla_prep(za3, tq, tk, qg, kg, wuq, wk, wv, tl):
    b, l, _ = za3.shape
    heads = pl.BlockSpec((None, H_A, tl, HEAD_W), lambda bi, i: (bi, 0, i, 0))
    tab = pl.BlockSpec((3, tl, LANE), lambda bi, i: (0, i, 0))
    hshape = jax.ShapeDtypeStruct((b, H_A, l, HEAD_W), wuq.dtype)
    return pl.pallas_call(
        _mla_prep_kernel, grid=(b, l // tl),
        in_specs=[pl.BlockSpec((None, tl, A_W), lambda bi, i: (bi, i, 0)), tab, tab,
                  _full(qg.shape), _full(kg.shape), _full(wuq.shape), _full(wk.shape), _full(wv.shape)],
        out_specs=[heads, heads, heads,
                   pl.BlockSpec((None, tl, KV_LORA), lambda bi, i: (bi, i, 0)),
                   pl.BlockSpec((None, tl, ROPE_A), lambda bi, i: (bi, i, 0))],
        out_shape=[hshape, hshape, hshape,
                   jax.ShapeDtypeStruct((b, l, KV_LORA), F32), jax.ShapeDtypeStruct((b, l, ROPE_A), F32)],
        compiler_params=_cparams("parallel", "parallel"), name="mla_prep")(za3, tq, tk, qg, kg, wuq, wk, wv)


def _mla_past_kernel(ckv_ref, kr_ref, wk_ref, wv_ref, kc_ref, v_ref):
    cb = ckv_ref[...]
    kc = _mm(cb, wk_ref[...]) + _tile_lanes(pltpu.roll(kr_ref[...], NOPE_A, 1), H_A)
    v = _mm(cb, wv_ref[...]) + _ones_lane(H_A * HEAD_W)
    for h in range(H_A):
        sl = slice(h * HEAD_W, (h + 1) * HEAD_W)
        kc_ref[h] = kc[:, sl].astype(kc_ref.dtype)
        v_ref[h] = v[:, sl].astype(v_ref.dtype)


def _mla_past(ckv, kr128, wk, wv, tl):
    b, p, _ = ckv.shape
    heads = pl.BlockSpec((None, H_A, tl, HEAD_W), lambda bi, i: (bi, 0, i, 0))
    hshape = jax.ShapeDtypeStruct((b, H_A, p, HEAD_W), wk.dtype)
    return pl.pallas_call(
        _mla_past_kernel, grid=(b, p // tl),
        in_specs=[pl.BlockSpec((None, tl, KV_LORA), lambda bi, i: (bi, i, 0)),
                  pl.BlockSpec((None, tl, LANE), lambda bi, i: (bi, i, 0)), _full(wk.shape), _full(wv.shape)],
        out_specs=[heads, heads], out_shape=[hshape, hshape],
        compiler_params=_cparams("parallel", "parallel"), name="mla_past")(ckv, kr128, wk, wv)


def _ones_lane(width):
    lane = lax.broadcasted_iota(I32, (1, width), 1)
    return jnp.where((lane & (HEAD_W - 1)) == V_A, 1.0, 0.0)


def _fold_lanes_max(s):
    out = s[:, :LANE]
    for c0 in range(LANE, s.shape[1], LANE):
        out = jnp.maximum(out, s[:, c0:c0 + LANE])
    return out


def _tile_groups(nt, masked):
    if isinstance(nt, int):
        unroll = TILE_UNROLL if nt % TILE_UNROLL == 0 else 1
        return nt // unroll, unroll
    assert masked, "a rounded-up tile count needs a mask"
    return (nt + TILE_UNROLL - 1) // TILE_UNROLL, TILE_UNROLL


def _group_offsets(g, u, unroll, tk, last_tile):
    j = g * unroll + u
    off = pl.multiple_of(j * tk, tk)
    if unroll == 1:
        return off, off
    return off, pl.multiple_of(jnp.minimum(j, last_tile) * tk, tk)


def _attend(q_ref, sources, s_scr, tq, n_heads, mask_fn):
    heads = range(n_heads)
    qs = [q_ref[h] for h in heads]
    mrun = tuple(jnp.full((tq, LANE), NEG_INF, F32) for _ in heads)
    for (k_ref, _, base, tk, nt, masked) in sources:
        groups, unroll = _tile_groups(nt, masked)
        last = k_ref.shape[1] // tk - 1

        def score_body(g, ms, k_ref=k_ref, base=base, tk=tk, masked=masked, unroll=unroll, last=last):
            for u in range(unroll):
                off, kv_off = _group_offsets(g, u, unroll, tk, last)
                col = pl.multiple_of(base + off, tk)
                ss = [_mm_t(qs[h], k_ref[h, pl.ds(kv_off, tk), :]) for h in heads]
                if masked:
                    mask = mask_fn(off, col, tk)
                    ss = [jnp.where(mask, s, NEG_INF) for s in ss]
                for h in heads:
                    s_scr[h, :, pl.ds(col, tk)] = ss[h]
                if tk % LANE == 0:
                    ms = tuple(jnp.maximum(ms[h], _fold_lanes_max(ss[h])) for h in heads)
                else:
                    ms = tuple(jnp.maximum(ms[h], jnp.max(ss[h], axis=1, keepdims=True)) for h in heads)
            return ms
        mrun = lax.fori_loop(0, groups, score_body, mrun)
    m = [jnp.max(mr, axis=1, keepdims=True) for mr in mrun]
    accs = tuple(jnp.zeros((tq, HEAD_W), F32) for _ in heads)
    for (_, v_ref, base, tk, nt, masked) in sources:
        groups, unroll = _tile_groups(nt, masked)
        last = v_ref.shape[1] // tk - 1

        def pv_body(g, acc, v_ref=v_ref, base=base, tk=tk, unroll=unroll, last=last):
            for u in range(unroll):
                off, kv_off = _group_offsets(g, u, unroll, tk, last)
                col = pl.multiple_of(base + off, tk)
                ps = [jnp.exp(s_scr[h, :, pl.ds(col, tk)] - m[h]) for h in heads]
                acc = tuple(acc[h] + _mm(ps[h], v_ref[h, pl.ds(kv_off, tk), :]) for h in heads)
            return acc
        accs = lax.fori_loop(0, groups, pv_body, accs)
    lane = lax.broadcasted_iota(I32, (1, HEAD_W), 1)
    return [jnp.where(lane < V_A, a / a[:, V_A:V_A + 1], 0.0) for a in accs]


def _mla_attn_kernel(*refs, tq, tkn, ln, causal, first_key, p_len, tkp):
    if p_len:
        q_ref, kp_ref, vp_ref, kn_ref, vn_ref, o_ref, s_scr = refs
    else:
        q_ref, kn_ref, vn_ref, o_ref, s_scr = refs
    qt = pl.program_id(1)
    if causal:
        row = qt * tq + lax.broadcasted_iota(I32, (tq, 1), 0)
        nvis = ((row >> 6) + 1) << 6
        n_tiles = (qt * tq + tq + tkn - 1) // tkn
    else:
        nvis = None
        n_tiles = ln // tkn
    sources = []
    if p_len:
        sources.append((kp_ref, vp_ref, 0, tkp, p_len // tkp, False))
    sources.append((kn_ref, vn_ref, p_len, tkn, n_tiles, causal))

    def mask_fn(off, col, tk):
        key = off + lax.broadcasted_iota(I32, (1, tk), 1)
        return (key >= first_key) & (key < nvis)

    o_ref[...] = _pair_heads(_attend(q_ref, sources, s_scr, tq, H_A, mask_fn))


def _mla_attn(q, kn, vn, past, tq, causal, first_key):
    b, _, l, _ = q.shape
    ln = kn.shape[2]
    tkn = KEY_TILE if ln % KEY_TILE == 0 else ln
    qspec = pl.BlockSpec((None, H_A, tq, HEAD_W), lambda bi, i: (bi, 0, i, 0))
    whole = lambda n: pl.BlockSpec((None, H_A, n, HEAD_W), lambda bi, i: (bi, 0, 0, 0))
    args, specs, p_len = [q], [qspec], 0
    if past is not None:
        p_len = past[0].shape[2]
        args += list(past)
        specs += [whole(p_len), whole(p_len)]
    args += [kn, vn]
    specs += [whole(ln), whole(ln)]
    kern = functools.partial(_mla_attn_kernel, tq=tq, tkn=tkn, ln=ln, causal=causal, first_key=first_key,
                             p_len=p_len, tkp=KEY_TILE)
    return pl.pallas_call(
        kern, grid=(b, l // tq), in_specs=specs,
        out_specs=pl.BlockSpec((None, tq, H_A * V_A), lambda bi, i: (bi, i, 0)),
        out_shape=jax.ShapeDtypeStruct((b, l, H_A * V_A), F32),
        scratch_shapes=[pltpu.VMEM((H_A, tq, p_len + _round_up(ln, TILE_UNROLL * tkn)), F32)],
        compiler_params=_cparams("parallel", "arbitrary"), name="mla_attn")(*args)


def _dsa_prep_kernel(zb_ref, tqk_ref, tiq_ref, tik_ref,
                     q_ref, k_ref, v_ref, qi_ref, ki_ref, wi_ref, kst_ref, vst_ref, kist_ref):
    zb = zb_ref[...]
    q = _rope(zb[:, B_Q:B_K], tqk_ref, DH_B // 2) * DSA_SCALE
    k = _rope(zb[:, B_K:B_V], tqk_ref, DH_B // 2)
    v = zb[:, B_V:B_QI]
    qi = _rope(zb[:, B_QI:B_KI], tiq_ref, D_I // 2)
    kiw = _rope(zb[:, B_KI:B_W], tik_ref, D_I // 2)
    ks, vs = [], []
    for h in range(H_B):
        sl = slice(h * HEAD_W, (h + 1) * HEAD_W)
        q_ref[h] = q[:, sl].astype(q_ref.dtype)
        k_ref[h] = k[:, sl].astype(k_ref.dtype)
        v_ref[h] = (v[:, sl] + _ones_lane(HEAD_W)).astype(v_ref.dtype)
        ks.append(k[:, sl])
        vs.append(v[:, sl])
    for h in range(H_I):
        qi_ref[h] = qi[:, h * D_I:(h + 1) * D_I].astype(qi_ref.dtype)
    ki_ref[...] = kiw[:, :D_I].astype(ki_ref.dtype)
    wi_ref[...] = kiw * IDX_SCALE
    kst_ref[...] = _pair_heads(ks)
    vst_ref[...] = _pair_heads(vs)
    kist_ref[...] = kiw[:, :D_I]


def _dsa_prep(zb3, tqk, tiq, tik, tl, mm_dtype):
    b, l, _ = zb3.shape
    heads = pl.BlockSpec((None, H_B, tl, HEAD_W), lambda bi, i: (bi, 0, i, 0))
    tab = pl.BlockSpec((3, tl, LANE), lambda bi, i: (0, i, 0))
    rows = lambda w: pl.BlockSpec((None, tl, w), lambda bi, i: (bi, i, 0))
    hshape = jax.ShapeDtypeStruct((b, H_B, l, HEAD_W), mm_dtype)
    return pl.pallas_call(
        _dsa_prep_kernel, grid=(b, l // tl),
        in_specs=[rows(B_W), tab, tab, tab],
        out_specs=[heads, heads, heads,
                   pl.BlockSpec((None, H_I, tl, D_I), lambda bi, i: (bi, 0, i, 0)),
                   rows(D_I), rows(LANE), rows(H_B * DH_B), rows(H_B * DH_B), rows(D_I)],
        out_shape=[hshape, hshape, hshape,
                   jax.ShapeDtypeStruct((b, H_I, l, D_I), mm_dtype),
                   jax.ShapeDtypeStruct((b, l, D_I), mm_dtype),
                   jax.ShapeDtypeStruct((b, l, LANE), F32),
                   jax.ShapeDtypeStruct((b, l, H_B * DH_B), F32),
                   jax.ShapeDtypeStruct((b, l, H_B * DH_B), F32),
                   jax.ShapeDtypeStruct((b, l, D_I), F32)],
        compiler_params=_cparams("parallel", "parallel"), name="dsa_prep")(zb3, tqk, tiq, tik)


def _order_key(x):
    x = jnp.where(x == 0.0, 0.0, x)
    b = pltpu.bitcast(x, I32)
    return jnp.where(b < 0, b ^ INT_MAX, b)


def _dsa_attn_kernel(*refs, tq, tkn, ln, causal, first_key, p_len, tkp, k_sel):
    if p_len:
        (q_ref, qi_ref, wi_ref, kp_ref, vp_ref, kip_ref, kn_ref, vn_ref, kin_ref, o_ref,
         keys_ref, s_scr, wrep_scr) = refs
    else:
        (q_ref, qi_ref, wi_ref, kn_ref, vn_ref, kin_ref, o_ref, keys_ref, s_scr, wrep_scr) = refs
    qt = pl.program_id(1)
    p_tiles = p_len // tkp if p_len else 0
    if causal:
        row = qt * tq + lax.broadcasted_iota(I32, (tq, 1), 0)
        nvis = ((row >> 6) + 1) << 6
        n_tiles = (qt * tq + tq + tkn - 1) // tkn
        n_valid = qt * tq + tq - first_key + p_len
    else:
        nvis = ln
        n_tiles = ln // tkn
        n_valid = ln + p_len

    sources = []
    if p_len:
        sources.append((kp_ref, vp_ref, kip_ref, 0, tkp, p_tiles, False))
    sources.append((kn_ref, vn_ref, kin_ref, p_len, tkn, n_tiles, causal))

    def col_mask(off, tk):
        col = off + lax.broadcasted_iota(I32, (1, tk), 1)
        return (col >= first_key) & (col < nvis)

    wi = wi_ref[...]
    qis = [qi_ref[h] for h in range(H_I)]
    for h in range(H_I):
        wrep_scr[h] = jnp.broadcast_to(wi[:, D_I + h:D_I + h + 1], (tq, LANE))
    for (_, _, ki_ref, base, tk, nt, masked) in sources:
        groups, unroll = _tile_groups(nt, masked)
        last = ki_ref.shape[0] // tk - 1

        def score_body(g, c, ki_ref=ki_ref, base=base, tk=tk, masked=masked, unroll=unroll, last=last):
            wts = [_tile_lanes(wrep_scr[h], tk // LANE) if tk >= LANE else wrep_scr[h][:, :tk] for h in range(H_I)]
            for u in range(unroll):
                off, kv_off = _group_offsets(g, u, unroll, tk, last)
                ki_t = ki_ref[pl.ds(kv_off, tk), :]
                sc = jnp.maximum(_mm_t(qis[0], ki_t), 0.0) * wts[0]
                for h in range(1, H_I):
                    sc = sc + jnp.maximum(_mm_t(qis[h], ki_t), 0.0) * wts[h]
                key = _order_key(sc)
                if masked:
                    key = jnp.where(col_mask(off, tk), key, INT_MIN)
                keys_ref[:, pl.ds(pl.multiple_of(base + off, tk), tk)] = key
            return c
        lax.fori_loop(0, groups, score_body, 0)

    count_sources = []
    for (_, _, _, base, tk, nt, _) in sources:
        if tk == KEY_TILE:
            count_sources.append((base, TILE_UNROLL * tk, _tile_groups(nt, True)[0] if not isinstance(nt, int)
                                  else nt // TILE_UNROLL))
        else:
            count_sources.append((base, tk, nt))

    def count_ge(mid):
        total = jnp.zeros((tq, 1), F32)
        for (base, tk, nt) in count_sources:
            wacc = min(tk, LANE)

            def cnt_body(j, acc, base=base, tk=tk, wacc=wacc):
                off = pl.multiple_of(base + j * tk, tk)
                hit = jnp.where(keys_ref[:, pl.ds(off, tk)] >= mid, 1.0, 0.0)
                for c0 in range(0, tk, wacc):
                    acc = acc + hit[:, c0:c0 + wacc]
                return acc
            acc = lax.fori_loop(0, nt, cnt_body, jnp.zeros((tq, wacc), F32))
            total = total + jnp.sum(acc, axis=1, keepdims=True)
        return total

    def bisect_body(_, c):
        lo, hi = c
        mid = (lo >> 1) + (hi >> 1) + (((lo & 1) + (hi & 1) + 1) >> 1)
        ge = count_ge(mid) >= float(k_sel)
        return jnp.where(ge, mid, lo), jnp.where(ge, hi, mid - 1)

    n_iter = jnp.where(n_valid > k_sel, 32, 0)
    thr, _ = lax.fori_loop(0, n_iter, bisect_body,
                           (jnp.full((tq, 1), INT_MIN + 1, I32), jnp.full((tq, 1), INT_MAX, I32)))

    n_sel = count_ge(thr)
    has_ties = jnp.max(jnp.where(n_sel > float(k_sel), 1.0, 0.0)) > 0.0

    @pl.when(has_ties)
    def _():
        keep = float(k_sel) - count_ge(thr + 1)
        run = jnp.zeros((tq, 1), F32)
        for (_, _, _, base, tk, nt, _) in sources:
            upper = jnp.where(lax.broadcasted_iota(I32, (tk, tk), 0) <= lax.broadcasted_iota(I32, (tk, tk), 1),
                              1.0, 0.0).astype(BF16)

            def tie_body(j, run, base=base, tk=tk, upper=upper):
                col = pl.multiple_of(base + j * tk, tk)
                kt = keys_ref[:, pl.ds(col, tk)]
                eq = kt == thr
                eqf = jnp.where(eq, 1.0, 0.0)
                rank = run + _mm(eqf, upper)
                keys_ref[:, pl.ds(col, tk)] = jnp.where(eq & (rank > keep), INT_MIN, kt)
                return run + jnp.sum(eqf, axis=1, keepdims=True)
            run = lax.fori_loop(0, nt, tie_body, run)

    def selected(off, col, tk):
        return keys_ref[:, pl.ds(col, tk)] >= thr

    att_sources = [(k_ref, v_ref, base, tk, nt, True) for (k_ref, v_ref, _, base, tk, nt, _) in sources]
    o_ref[...] = _pair_heads(_attend(q_ref, att_sources, s_scr, tq, H_B, selected))


def _dsa_attn(q, qi, wi, kn, vn, kin, past, tq, causal, first_key, k_sel):
    b, _, l, _ = q.shape
    ln = kn.shape[2]
    tkn = KEY_TILE if ln % KEY_TILE == 0 else ln
    qspec = pl.BlockSpec((None, H_B, tq, HEAD_W), lambda bi, i: (bi, 0, i, 0))
    whole = lambda n: pl.BlockSpec((None, H_B, n, HEAD_W), lambda bi, i: (bi, 0, 0, 0))
    whole_ki = lambda n: pl.BlockSpec((None, n, D_I), lambda bi, i: (bi, 0, 0))
    args = [q, qi, wi]
    specs = [qspec, pl.BlockSpec((None, H_I, tq, D_I), lambda bi, i: (bi, 0, i, 0)),
             pl.BlockSpec((None, tq, LANE), lambda bi, i: (bi, i, 0))]
    p_len = 0
    if past is not None:
        p_len = past[0].shape[2]
        args += list(past)
        specs += [whole(p_len), whole(p_len), whole_ki(p_len)]
    args += [kn, vn, kin]
    specs += [whole(ln), whole(ln), whole_ki(ln)]
    kern = functools.partial(_dsa_attn_kernel, tq=tq, tkn=tkn, ln=ln, causal=causal, first_key=first_key,
                             p_len=p_len, tkp=KEY_TILE, k_sel=k_sel)
    return pl.pallas_call(
        kern, grid=(b, l // tq), in_specs=specs,
        out_specs=pl.BlockSpec((None, tq, H_B * DH_B), lambda bi, i: (bi, i, 0)),
        out_shape=jax.ShapeDtypeStruct((b, l, H_B * DH_B), F32),
        scratch_shapes=[pltpu.VMEM((tq, p_len + _round_up(ln, TILE_UNROLL * tkn)), I32),
                        pltpu.VMEM((H_B, tq, p_len + _round_up(ln, TILE_UNROLL * tkn)), F32),
                        pltpu.VMEM((H_I, tq, LANE), F32)],
        compiler_params=_cparams("parallel", "arbitrary"), name="dsa_attn")(*args)


def _softplus(x):
    return jnp.maximum(x, 0.0) + jnp.log(1.0 + jnp.exp(-jnp.abs(x)))


def _rwkv_prep_kernel(zc_ref, zp_ref, sh_ref, mu_ref, w0_ref, a0_ref, kkw_ref, ka_ref,
                      wup_ref, aup_ref, gup_ref, bd_ref,
                      r_ref, k_ref, v_ref, ld_ref, kk_ref, b_ref, g_ref, *, tl, lo, hi):
    i = pl.program_id(1)
    z = zc_ref[...]
    rowi = lax.broadcasted_iota(I32, (tl, 1), 0)
    prev_last = jnp.where(i == 0, sh_ref[...], zp_ref[7:8, :])
    zprev = jnp.where(rowi == 0, prev_last, pltpu.roll(z, 1, 0))
    zs = z + (zprev - z) * mu_ref[...]
    r, k, v, lora = zs[:, :W_C], zs[:, W_C:2 * W_C], zs[:, 2 * W_C:3 * W_C], zs[:, 3 * W_C:]
    lane = lax.broadcasted_iota(I32, (1, LANE), 1)
    u = jnp.where(lane < W_LORA, jnp.tanh(lora),
                  jnp.where(lane < W_LORA + A_LORA, lora, jax.nn.sigmoid(lora)))
    w = -_softplus(-(w0_ref[...] + _mm(u, wup_ref[...]))) - 0.5
    ld = -jnp.exp(w)
    a = jax.nn.sigmoid(a0_ref[...] + _mm(u, aup_ref[...]))
    g = _mm(u, gup_ref[...])
    kmod = k * (1.0 + (a - 1.0) * ka_ref[...])
    kk = k * kkw_ref[...]
    sq = kk * kk
    bd = bd_ref[...]
    if bd.dtype == F32:
        ss = _mm(sq, bd)
    else:
        sq_hi = sq.astype(BF16)
        ss = _mm(sq_hi, bd) + _mm(sq - sq_hi.astype(F32), bd)
    kkn = kk * jnp.minimum(lax.rsqrt(ss), 1e12)
    grow = i * tl + rowi
    valid = ((grow >= lo) & (grow < hi)).astype(F32)
    r_ref[...] = r
    k_ref[...] = kmod * valid
    v_ref[...] = v * valid
    ld_ref[...] = ld * valid
    kk_ref[...] = kkn * valid
    b_ref[...] = kkn * a * valid
    g_ref[...] = g


def _rwkv_prep(zc3, shift, p, tl, lo, hi):
    b, l, _ = zc3.shape
    rows = lambda w: pl.BlockSpec((None, tl, w), lambda bi, i: (bi, i, 0))
    vec = _full((1, W_C))
    mat = _full((LANE, W_C))
    kern = functools.partial(_rwkv_prep_kernel, tl=tl, lo=lo, hi=hi)
    return pl.pallas_call(
        kern, grid=(b, l // tl),
        in_specs=[rows(C_W),
                  pl.BlockSpec((None, 8, C_W), lambda bi, i: (bi, jnp.maximum(i * (tl // 8) - 1, 0), 0)),
                  pl.BlockSpec((None, 1, C_W), lambda bi, i: (bi, 0, 0)),
                  _full((1, C_W)), vec, vec, vec, vec, mat, mat, mat, _full((W_C, W_C))],
        out_specs=[rows(W_C)] * 7,
        out_shape=[jax.ShapeDtypeStruct((b, l, W_C), F32)] * 7,
        compiler_params=_cparams("parallel", "parallel"), name="rwkv_prep")(
            zc3, zc3, shift, p['mu'], p['w0'], p['a0'], p['k_k'], p['k_a'], p['wup'], p['aup'], p['gup'], p['bd'])


def _rwkv_scan_kernel(r_ref, k_ref, v_ref, ld_ref, kk_ref, b_ref, g_ref, s0_ref, lng_ref, lnb_ref, rk_ref,
                      o_ref, sout_ref, s_scr, *, c, n_chunks, md, nb):
    ci = pl.program_id(1)

    @pl.when(ci == 0)
    def _():
        s_scr[...] = s0_ref[...]

    rowi = lax.broadcasted_iota(I32, (c, 1), 0)
    ri = lax.broadcasted_iota(I32, (c, c), 0)
    cj = lax.broadcasted_iota(I32, (c, c), 1)
    strict, incl = ri > cj, ri >= cj
    eye = jnp.where(ri == cj, 1.0, 0.0)
    n_sq = c.bit_length() - 2

    rows = []
    for bi in range(nb):
        ld = ld_ref[bi]
        cum = ld
        step = 1
        while step < c:
            cum = cum + jnp.where(rowi >= step, pltpu.roll(cum, step, 0), 0.0)
            step *= 2
        cl = cum[c - 1:c, :]
        e_in, e_ex, e_neg, e_end = jnp.exp(cum), jnp.exp(cum - ld), jnp.exp(-cum), jnp.exp(cl - cum)
        r, k, v, kk, bb = r_ref[bi], k_ref[bi], v_ref[bi], kk_ref[bi], b_ref[bi]
        rows.append(dict(d_end=jnp.exp(cl), v=v, rt=r * e_in, kap=kk * e_ex, kt=k * e_neg, bt=bb * e_neg,
                         kte=k * e_end, bte=bb * e_end, bonus_rk=r * k * rk_ref[...]))

    units = [(bi, h) for bi in range(nb) for h in range(H_C)]
    idx = range(len(units))
    sls = [slice(h * N_C, (h + 1) * N_C) for _, h in units]
    s_old = [s_scr[bi, h] for bi, h in units]
    s_mm = [s.astype(md) for s in s_old]
    left = [jnp.concatenate([rows[bi]['kap'][:, sls[u]], rows[bi]['rt'][:, sls[u]]], axis=0).astype(md)
            for u, (bi, _) in enumerate(units)]
    right = [jnp.concatenate([rows[bi]['bt'][:, sls[u]], rows[bi]['kt'][:, sls[u]]], axis=0).astype(md)
             for u, (bi, _) in enumerate(units)]
    gram = [_mm_t(left[u], right[u]) for u in idx]
    a_kb = [jnp.where(strict, g[:c, :c], 0.0) for g in gram]
    a_kk = [jnp.where(strict, g[:c, c:], 0.0) for g in gram]
    a_rb = [jnp.where(incl, g[c:, :c], 0.0) for g in gram]
    a_rk = [jnp.where(incl, g[c:, c:], 0.0) for g in gram]
    vh = [rows[bi]['v'][:, sls[u]] for u, (bi, _) in enumerate(units)]
    vb = [x.astype(md) for x in vh]
    x0 = [_mm_t(left[u][:c], s_mm[u]) + _mm(a_kk[u], vb[u]) for u in idx]
    y0 = [_mm_t(left[u][c:], s_mm[u]) + _mm(a_rk[u], vb[u]) for u in idx]
    tinv = [eye - n for n in a_kb]
    npow = a_kb
    for _ in range(n_sq):
        npow_mm = [n.astype(md) for n in npow]
        npow = [_mm(n, n) for n in npow_mm]
        tinv = [tinv[u] + _mm(tinv[u], npow[u].astype(md)) for u in idx]
    w = [_mm(tinv[u], x0[u].astype(md)) for u in idx]
    wb = [x.astype(md) for x in w]
    y = [y0[u] - _mm(a_rb[u], wb[u]) for u in idx]
    for u, (bi, h) in enumerate(units):
        sl = sls[u]
        s_scr[bi, h] = (s_old[u] * rows[bi]['d_end'][:, sl] + _mm(vh[u].T, rows[bi]['kte'][:, sl].astype(md))
                        - _mm(w[u].T, rows[bi]['bte'][:, sl].astype(md)))
    for u, (bi, h) in enumerate(units):
        sl = sls[u]
        ym = jnp.mean(y[u], axis=-1, keepdims=True)
        yc = y[u] - ym
        yn = yc * lax.rsqrt(jnp.mean(yc * yc, axis=-1, keepdims=True) + GN_EPS) * lng_ref[:, sl] + lnb_ref[:, sl]
        bonus = jnp.sum(rows[bi]['bonus_rk'][:, sl], axis=-1, keepdims=True) * vh[u]
        o_ref[bi, :, sl] = (yn + bonus) * g_ref[bi, :, sl]

    @pl.when(ci == n_chunks - 1)
    def _():
        sout_ref[...] = s_scr[...]


def _rwkv_scan(pre, s0, p, c, mm_dtype):
    r = pre[0]
    b, l, _ = r.shape
    n_chunks = l // c
    nb = 2 if b % 2 == 0 else 1
    rows = pl.BlockSpec((nb, c, W_C), lambda bi, i: (bi, i, 0))
    state = pl.BlockSpec((nb, H_C, N_C, N_C), lambda bi, i: (bi, 0, 0, 0))
    vec = _full((1, W_C))
    kern = functools.partial(_rwkv_scan_kernel, c=c, n_chunks=n_chunks, md=mm_dtype, nb=nb)
    return pl.pallas_call(
        kern, grid=(b // nb, n_chunks),
        in_specs=[rows] * 7 + [state, vec, vec, vec],
        out_specs=[rows, state],
        out_shape=[jax.ShapeDtypeStruct((b, l, W_C), F32), jax.ShapeDtypeStruct((b, H_C, N_C, N_C), F32)],
        scratch_shapes=[pltpu.VMEM((nb, H_C, N_C, N_C), F32)],
        compiler_params=_cparams("parallel", "arbitrary"), name="rwkv_scan")(
            *pre, s0, p['ln_g'], p['ln_b'], p['r_k'])


def _layernorm(y, g, b):
    mu = jnp.mean(y, axis=-1, keepdims=True)
    yc = y - mu
    return yc * lax.rsqrt(jnp.mean(yc * yc, axis=-1, keepdims=True) + LN_EPS) * g + b


def _row_valid(i, tm, tiles_per_batch, lo, hi):
    rowb = (i % tiles_per_batch) * tm + lax.broadcasted_iota(I32, (tm, 1), 0)
    return (rowb >= lo) & (rowb < hi)


def _outproj_kernel(oa_ref, ob_ref, oc_ref, x_ref, wa_ref, wb_ref, wc_ref, g_ref, b_ref, y_ref,
                    *, tm, tiles_per_batch, lo, hi, masked):
    mix = _mm(oa_ref[...], wa_ref[...]) + _mm(ob_ref[...], wb_ref[...]) + _mm(oc_ref[...], wc_ref[...])
    y = _layernorm(ALPHA * x_ref[...] + mix, g_ref[...], b_ref[...])
    if masked:
        y = jnp.where(_row_valid(pl.program_id(0), tm, tiles_per_batch, lo, hi), y, 0.0)
    y_ref[...] = y


def _outproj(oa, ob, oc, x2d, p, tm, rows_per_batch, lo, hi, masked):
    t = x2d.shape[0]
    row = lambda w: pl.BlockSpec((tm, w), lambda i: (i, 0))
    kern = functools.partial(_outproj_kernel, tm=tm, tiles_per_batch=rows_per_batch // tm, lo=lo, hi=hi,
                             masked=masked)
    return pl.pallas_call(
        kern, grid=(t // tm,),
        in_specs=[row(oa.shape[1]), row(ob.shape[1]), row(oc.shape[1]), row(D_MODEL),
                  _full(p['wo_a'].shape), _full(p['wo_b'].shape), _full(p['wo_c'].shape),
                  _full((1, D_MODEL)), _full((1, D_MODEL))],
        out_specs=row(D_MODEL), out_shape=jax.ShapeDtypeStruct((t, D_MODEL), F32),
        compiler_params=_cparams("parallel"), name="outproj_ln")(
            oa, ob, oc, x2d, p['wo_a'], p['wo_b'], p['wo_c'], p['ln1_g'], p['ln1_b'])


def _router_kernel(x_ref, whi_ref, wlo_ref, br_ref, dg_ref):
    x = x_ref[...]
    xh = x.astype(BF16)
    xl = (x - xh.astype(F32)).astype(BF16)
    logits = _mm(xh, whi_ref[...]) + _mm(xl, whi_ref[...]) + _mm(xh, wlo_ref[...]) + br_ref[...]
    lane = lax.broadcasted_iota(I32, (1, LANE), 1).astype(F32)
    big = float(LANE)
    gm = (lane >= N_EXPERTS) & (lane < N_EXPERTS + N_GROUPS)
    gl = jnp.where(gm, logits, -jnp.inf)
    ge = jnp.exp(gl - jnp.max(gl, axis=1, keepdims=True))
    gp = ge / jnp.sum(ge, axis=1, keepdims=True)
    gval = jnp.max(gp, axis=1, keepdims=True)
    gidx = jnp.min(jnp.where(gm & (gp == gval), lane, big), axis=1, keepdims=True) - N_EXPERTS
    elo = gidx * EXP_PER_GROUP
    em = (lane >= elo) & (lane < elo + EXP_PER_GROUP)
    el = jnp.where(em, logits, -jnp.inf)
    ee = jnp.exp(el - jnp.max(el, axis=1, keepdims=True))
    ep = jnp.where(em, ee / jnp.sum(ee, axis=1, keepdims=True), -1.0)
    p1 = jnp.max(ep, axis=1, keepdims=True)
    i1 = jnp.min(jnp.where(ep == p1, lane, big), axis=1, keepdims=True)
    ep2 = jnp.where(lane == i1, -1.0, ep)
    p2 = jnp.max(ep2, axis=1, keepdims=True)
    i2 = jnp.min(jnp.where((ep2 == p2) & (lane != i1), lane, big), axis=1, keepdims=True)
    den = p1 + p2
    dg_ref[...] = jnp.where(lane == i1, gval * p1 / den, 0.0) + jnp.where(lane == i2, gval * p2 / den, 0.0)


def _router(x2d, p, tm):
    t = x2d.shape[0]
    row = lambda w: pl.BlockSpec((tm, w), lambda i: (i, 0))
    return pl.pallas_call(
        _router_kernel, grid=(t // tm,),
        in_specs=[row(D_MODEL), _full((D_MODEL, LANE)), _full((D_MODEL, LANE)), _full((1, LANE))],
        out_specs=row(LANE), out_shape=jax.ShapeDtypeStruct((t, LANE), F32),
        compiler_params=_cparams("parallel"), name="router")(x2d, p['wr_hi'], p['wr_lo'], p['br'])


def _moe_kernel(x_ref, dg_ref, wgu_ref, wd_ref, g_ref, b_ref, y_ref, acc_ref, xb_ref,
                *, tm, tiles_per_batch, lo, hi, masked, fused_down):
    e = pl.program_id(1)

    @pl.when(e == 0)
    def _():
        xb_ref[...] = x_ref[...].astype(xb_ref.dtype)
        if not fused_down:
            acc_ref[...] = jnp.zeros_like(acc_ref)

    hcat = _mm(xb_ref[...], wgu_ref[...])
    hg, hu = hcat[:, :D_EXPERT], hcat[:, D_EXPERT:]
    lane = lax.broadcasted_iota(I32, (1, LANE), 1)
    gate = jnp.sum(jnp.where(lane == e, dg_ref[...], 0.0), axis=1, keepdims=True)
    hidden = hg * jax.nn.sigmoid(hg) * hu * gate
    if fused_down:
        acc_ref[:, pl.ds(pl.multiple_of(e * D_EXPERT, D_EXPERT), D_EXPERT)] = hidden.astype(acc_ref.dtype)
    else:
        acc_ref[...] += _mm(hidden, wd_ref[...])

    @pl.when(e == N_EXPERTS - 1)
    def _():
        ffn = _mm(acc_ref[...], wd_ref[...]) if fused_down else acc_ref[...]
        y = _layernorm(ALPHA * x_ref[...] + ffn, g_ref[...], b_ref[...])
        if masked:
            y = jnp.where(_row_valid(pl.program_id(0), tm, tiles_per_batch, lo, hi), y, 0.0)
        y_ref[...] = y


def _moe(x2d, dg, p, tm, rows_per_batch, lo, hi, masked):
    t = x2d.shape[0]
    row = lambda w: pl.BlockSpec((tm, w), lambda i, e: (i, 0))
    md = p['wgu'].dtype
    fused_down = md == BF16
    kern = functools.partial(_moe_kernel, tm=tm, tiles_per_batch=rows_per_batch // tm, lo=lo, hi=hi,
                             masked=masked, fused_down=fused_down)
    if fused_down:
        wd = p['wd'].reshape(N_EXPERTS * D_EXPERT, D_MODEL)
        wd_spec = pl.BlockSpec(wd.shape, lambda i, e: (0, 0), pipeline_mode=pl.Buffered(1))
        acc = pltpu.VMEM((tm, N_EXPERTS * D_EXPERT), md)
    else:
        wd = p['wd']
        wd_spec = pl.BlockSpec((None, D_EXPERT, D_MODEL), lambda i, e: (e, 0, 0))
        acc = pltpu.VMEM((tm, D_MODEL), F32)
    return pl.pallas_call(
        kern, grid=(t // tm, N_EXPERTS),
        in_specs=[row(D_MODEL), row(LANE),
                  pl.BlockSpec((None, D_MODEL, 2 * D_EXPERT), lambda i, e: (e, 0, 0)), wd_spec,
                  pl.BlockSpec((1, D_MODEL), lambda i, e: (0, 0)), pl.BlockSpec((1, D_MODEL), lambda i, e: (0, 0))],
        out_specs=row(D_MODEL), out_shape=jax.ShapeDtypeStruct((t, D_MODEL), F32),
        scratch_shapes=[acc, pltpu.VMEM((tm, D_MODEL), md)],
        compiler_params=_cparams("parallel", "arbitrary"), name="moe_ln")(
            x2d, dg, p['wgu'], wd, p['ln2_g'], p['ln2_b'])


def _rope_table(pos, pattern):
    posf = pos.astype(F32)[:, None]
    n = pos.shape[0]
    cos, s_lo, s_hi = [], [], []
    for kind, w in pattern:
        if kind == 'rope':
            half = w // 2
            inv = ROPE_THETA ** (-jnp.arange(half, dtype=F32) / half)
            ang = posf * inv[None, :]
            c, s, z = jnp.cos(ang), jnp.sin(ang), jnp.zeros((n, half), F32)
            cos += [c, c]
            s_lo += [-s, z]
            s_hi += [z, s]
        else:
            fill = jnp.full((n, w), 1.0 if kind == 'one' else 0.0, F32)
            z = jnp.zeros((n, w), F32)
            cos.append(fill)
            s_lo.append(z)
            s_hi.append(z)
    return jnp.stack([jnp.concatenate(t, axis=1) for t in (cos, s_lo, s_hi)])


def _tables(pos):
    return {
        'mla_q': _rope_table(pos, [('one', NOPE_A), ('rope', ROPE_A), ('zero', HEAD_W - NOPE_A - ROPE_A)]),
        'mla_kr': _rope_table(pos, [('rope', ROPE_A), ('zero', LANE - ROPE_A)]),
        'dsa_qk': _rope_table(pos, [('rope', DH_B), ('zero', HEAD_W - DH_B)]),
        'idx_q': _rope_table(pos, [('rope', D_I)] * (LANE // D_I)),
        'idx_k': _rope_table(pos, [('rope', D_I), ('one', H_I), ('zero', LANE - D_I - H_I)]),
    }


def _pad_cols(w, width):
    return jnp.pad(w, [(0, 0)] * (w.ndim - 1) + [(0, width - w.shape[-1])])


def _head_pad(w, n_heads, width):
    d = w.shape[-1] // n_heads
    w = w.reshape(w.shape[:-1] + (n_heads, d))
    return _pad_cols(w, width).reshape(w.shape[:-2] + (n_heads * width,))


def _prepare_weights(w, md):
    w_in = w['w_in']
    a, bseg, cseg = w_in[..., :A_COLS], w_in[..., A_COLS:A_COLS + B_COLS], w_in[..., A_COLS + B_COLS:]
    hd = H_B * DH_B
    wb = jnp.concatenate([
        _head_pad(bseg[..., 0:hd], H_B, HEAD_W), _head_pad(bseg[..., hd:2 * hd], H_B, HEAD_W),
        _head_pad(bseg[..., 2 * hd:3 * hd], H_B, HEAD_W), bseg[..., 3 * hd:3 * hd + H_I * D_I],
        _pad_cols(bseg[..., 3 * hd + H_I * D_I:], LANE)], axis=-1)
    w_uq = w['mla_w_uq'].reshape(DEPTH, Q_LORA, H_A, NOPE_A + ROPE_A)
    w_ukv = w['mla_w_ukv'].reshape(DEPTH, KV_LORA, H_A, NOPE_A + V_A)
    zeros_lora = lambda n: jnp.zeros((DEPTH, n, W_C), F32)
    head_of = jnp.arange(W_C) // N_C
    wr = jnp.concatenate([w['router_expert'], w['router_group']], axis=-1)
    wr = _pad_cols(wr, LANE)
    wr_hi = wr.astype(BF16)
    row = lambda v: v.reshape(DEPTH, 1, -1)
    return {
        'wa': _pad_cols(a, A_W).astype(md), 'wb': wb.astype(md), 'wc': cseg.astype(md),
        'qg': row(w['mla_q_norm']), 'kg': row(w['mla_kv_norm']),
        'wuq': _pad_cols(w_uq, HEAD_W).reshape(DEPTH, Q_LORA, H_A * HEAD_W).astype(md),
        'wk': _pad_cols(w_ukv[..., :NOPE_A], HEAD_W).reshape(DEPTH, KV_LORA, H_A * HEAD_W).astype(md),
        'wv': _pad_cols(w_ukv[..., NOPE_A:], HEAD_W).reshape(DEPTH, KV_LORA, H_A * HEAD_W).astype(md),
        'mu': row(w['rwkv_mu']), 'w0': row(w['rwkv_w0']), 'a0': row(w['rwkv_a0']),
        'k_k': row(w['rwkv_k_k']), 'k_a': row(w['rwkv_k_a']), 'r_k': row(w['rwkv_r_k']),
        'ln_g': row(w['rwkv_ln_g']), 'ln_b': row(w['rwkv_ln_b']),
        'wup': jnp.concatenate([w['rwkv_w_up'], zeros_lora(LANE - W_LORA)], axis=1).astype(md),
        'aup': jnp.concatenate([zeros_lora(W_LORA), w['rwkv_a_up'], zeros_lora(G_LORA)], axis=1).astype(md),
        'gup': jnp.concatenate([zeros_lora(W_LORA + A_LORA), w['rwkv_g_up']], axis=1).astype(md),
        'bd': jnp.broadcast_to((head_of[:, None] == head_of[None, :]).astype(md), (DEPTH, W_C, W_C)),
        'wo_a': w['w_out'][:, :H_A * V_A].astype(md),
        'wo_b': w['w_out'][:, H_A * V_A:H_A * V_A + hd].astype(md),
        'wo_c': w['w_out'][:, H_A * V_A + hd:].astype(md),
        'ln1_g': row(w['ln1_g']), 'ln1_b': row(w['ln1_b']), 'ln2_g': row(w['ln2_g']), 'ln2_b': row(w['ln2_b']),
        'wr_hi': wr_hi, 'wr_lo': (wr - wr_hi.astype(F32)).astype(BF16),
        'br': _pad_cols(jnp.concatenate([w['router_expert_b'], w['router_group_b']], axis=-1), LANE).reshape(DEPTH, 1, LANE),
        'wgu': jnp.concatenate([w['exp_w_gate'], w['exp_w_up']], axis=-1).astype(md),
        'wd': w['exp_w_down'].astype(md),
    }


def _largest_tile(n, limit):
    best = 8
    for t in range(8, min(n, limit) + 1, 8):
        if n % t == 0:
            best = t
    return best


def _trunk(x, pos, past, wp, *, lo, hi, causal, k_sel):
    b, l, _ = x.shape
    t = b * l
    tabs = _tables(pos)
    tl = min(256, l)
    tq_mla = min(4 * CHUNK, l)
    tq_dsa = min(2 * CHUNK, l)
    c_rwkv = min(CHUNK, l)
    masked = not (lo == 0 and hi == l)
    md = wp['wa'].dtype
    tm_cap = 512 if md == BF16 else 128
    tm = _largest_tile(l, tm_cap) if masked else _largest_tile(t, tm_cap)
    tm_moe = _largest_tile(l, 1100) if masked else _largest_tile(t, 1100)
    first_key = lo if causal else 0
    new = {k: [] for k in ('ckv', 'krope', 'dsa_k', 'dsa_v', 'dsa_kidx', 'rwkv', 'shift')}
    x2d = x.reshape(t, D_MODEL)
    for layer in range(DEPTH):
        p = {k: v[layer] for k, v in wp.items()}
        za, zb, zc = _proj_in(x2d, p['wa'], p['wb'], p['wc'], tm)
        za3, zb3, zc3 = za.reshape(b, l, A_W), zb.reshape(b, l, B_W), zc.reshape(b, l, C_W)

        q, kc, v, ckv_new, kr_new = _mla_prep(za3, tabs['mla_q'], tabs['mla_kr'], p['qg'], p['kg'],
                                              p['wuq'], p['wk'], p['wv'], tl)
        mla_past = None
        if past is not None:
            mla_past = _mla_past(past['ckv'][layer], _pad_cols(past['krope'][layer], LANE), p['wk'], p['wv'], 512)
        oa = _mla_attn(q, kc, v, mla_past, tq_mla, causal, first_key)

        qd, kd, vd, qi, ki, wi, kst, vst, kist = _dsa_prep(zb3, tabs['dsa_qk'], tabs['idx_q'], tabs['idx_k'], tl, md)
        dsa_past = None
        if past is not None:
            to_heads = lambda c: _pad_cols(jnp.swapaxes(c.astype(md), 1, 2), HEAD_W)
            ones_col = (jnp.arange(HEAD_W) == DH_B).astype(md)
            dsa_past = (to_heads(past['dsa_k'][layer]), to_heads(past['dsa_v'][layer]) + ones_col,
                        past['dsa_kidx'][layer].astype(md))
        ob = _dsa_attn(qd, qi, wi, kd, vd, ki, dsa_past, tq_dsa, causal, first_key, k_sel)

        if past is not None:
            shift, s0 = past['shift'][layer][:, None, :], past['rwkv'][layer]
        else:
            shift, s0 = jnp.zeros((b, 1, C_W), F32), jnp.zeros((b, H_C, N_C, N_C), F32)
        pre = _rwkv_prep(zc3, shift, p, tl, lo, hi)
        oc, s_last = _rwkv_scan(pre, s0, p, c_rwkv, md)

        x1 = _outproj(oa.reshape(t, -1), ob.reshape(t, -1), oc.reshape(t, -1), x2d, p, tm, l, lo, hi, masked)
        dg = _router(x1, p, tm_moe)
        x2d = _moe(x1, dg, p, tm_moe, l, lo, hi, masked)

        n = hi - lo
        new['ckv'].append(ckv_new[:, lo:hi])
        new['krope'].append(kr_new[:, lo:hi])
        new['dsa_k'].append(kst[:, lo:hi].reshape(b, n, H_B, DH_B))
        new['dsa_v'].append(vst[:, lo:hi].reshape(b, n, H_B, DH_B))
        new['dsa_kidx'].append(kist[:, lo:hi])
        new['rwkv'].append(s_last)
        new['shift'].append(zc3[:, hi - 1, :])
    return x2d.reshape(b, l, D_MODEL), {k: jnp.stack(v) for k, v in new.items()}


def kernel(x_prompt, x_sample, cache_mla_ckv, cache_mla_krope, cache_dsa_k, cache_dsa_v, cache_dsa_kidx, state_rwkv, state_rwkv_shift, meta_tokens, w_in, mla_q_norm, mla_kv_norm, mla_w_uq, mla_w_ukv, rwkv_mu, rwkv_w0, rwkv_w_up, rwkv_a0, rwkv_a_up, rwkv_g_up, rwkv_k_k, rwkv_k_a, rwkv_r_k, rwkv_ln_g, rwkv_ln_b, w_out, ln1_g, ln1_b, ln2_g, ln2_b, router_group, router_group_b, router_expert, router_expert_b, exp_w_gate, exp_w_up, exp_w_down):
    weights = {
        'w_in': w_in, 'mla_q_norm': mla_q_norm, 'mla_kv_norm': mla_kv_norm, 'mla_w_uq': mla_w_uq,
        'mla_w_ukv': mla_w_ukv, 'rwkv_mu': rwkv_mu, 'rwkv_w0': rwkv_w0, 'rwkv_w_up': rwkv_w_up,
        'rwkv_a0': rwkv_a0, 'rwkv_a_up': rwkv_a_up, 'rwkv_g_up': rwkv_g_up, 'rwkv_k_k': rwkv_k_k,
        'rwkv_k_a': rwkv_k_a, 'rwkv_r_k': rwkv_r_k, 'rwkv_ln_g': rwkv_ln_g, 'rwkv_ln_b': rwkv_ln_b,
        'w_out': w_out, 'ln1_g': ln1_g, 'ln1_b': ln1_b, 'ln2_g': ln2_g, 'ln2_b': ln2_b,
        'router_group': router_group, 'router_group_b': router_group_b, 'router_expert': router_expert,
        'router_expert_b': router_expert_b, 'exp_w_gate': exp_w_gate, 'exp_w_up': exp_w_up,
        'exp_w_down': exp_w_down,
    }
    wp = _prepare_weights(weights, BF16)
    wp_sample = _prepare_weights(weights, F32)

    bp, seq, _ = x_prompt.shape
    lo, hi = FRONT_PAD, FRONT_PAD + N_META + seq
    lp = -(-hi // KEY_TILE) * KEY_TILE
    meta = jnp.broadcast_to(meta_tokens.astype(F32)[None], (bp, N_META, D_MODEL))
    xp = jnp.concatenate([jnp.zeros((bp, lo, D_MODEL), F32), meta, x_prompt,
                          jnp.zeros((bp, lp - hi, D_MODEL), F32)], axis=1)
    pos_p = jnp.arange(lp, dtype=jnp.int32) - lo
    yp, new_p = _trunk(xp, pos_p, None, wp, lo=lo, hi=hi, causal=True, k_sel=min(TOPK_MAX, seq // 4))
    y_prompt = yp[:, lo + N_META:hi]

    n_past, n_new = cache_mla_ckv.shape[2], x_sample.shape[1]
    assert n_past % (TILE_UNROLL * KEY_TILE) == 0 and n_new <= CHUNK and n_new % 8 == 0, (n_past, n_new)
    pos_s = n_past + jnp.arange(n_new, dtype=jnp.int32)
    past_s = {'ckv': cache_mla_ckv, 'krope': cache_mla_krope, 'dsa_k': cache_dsa_k, 'dsa_v': cache_dsa_v,
              'dsa_kidx': cache_dsa_kidx, 'rwkv': state_rwkv, 'shift': state_rwkv_shift}
    y_sample, new_s = _trunk(x_sample, pos_s, past_s, wp_sample, lo=0, hi=n_new, causal=False,
                             k_sel=min(TOPK_MAX, (n_past + n_new) // 4))

    keys = ('ckv', 'krope', 'dsa_k', 'dsa_v', 'dsa_kidx', 'rwkv', 'shift')
    return (y_prompt, y_sample) + tuple(new_p[k] for k in keys) + tuple(new_s[k] for k in keys)
```

```python
import functools

import jax
import jax.numpy as jnp
from jax import lax
from jax.experimental import pallas as pl
from jax.experimental.pallas import tpu as pltpu

F32 = jnp.float32
BF16 = jnp.bfloat16
I32 = jnp.int32

D_MODEL = 1024
DEPTH = 4
CHUNK = 64
N_META = 16
ROPE_THETA = 10000.0
NEG_INF = -1e30
LN_EPS = 1e-5
RMS_EPS = 1e-6
GN_EPS = 64e-5
ALPHA = (2 * DEPTH) ** 0.25

H_A, Q_LORA, KV_LORA, NOPE_A, ROPE_A, V_A = 4, 256, 128, 64, 32, 64
H_B, DH_B, H_I, D_I, TOPK_MAX = 4, 64, 8, 32, 256
H_C, N_C, W_C, W_LORA, A_LORA, G_LORA = 8, 64, 512, 32, 32, 64
N_GROUPS, EXP_PER_GROUP, N_EXPERTS, D_EXPERT = 4, 4, 16, 256

A_COLS = Q_LORA + KV_LORA + ROPE_A
B_COLS = 3 * H_B * DH_B + H_I * D_I + D_I + H_I
C_COLS = 3 * W_C + W_LORA + A_LORA + G_LORA

MLA_SCALE = (NOPE_A + ROPE_A) ** -0.5
DSA_SCALE = DH_B ** -0.5
IDX_SCALE = (H_I ** -0.5) * (D_I ** -0.5)

LANE = 128
HEAD_W = 128
A_W = Q_LORA + KV_LORA + LANE
B_Q, B_K, B_V = 0, H_B * HEAD_W, 2 * H_B * HEAD_W
B_QI = 3 * H_B * HEAD_W
B_KI = B_QI + H_I * D_I
B_W = B_KI + LANE
C_W = C_COLS
FRONT_PAD = CHUNK - N_META
KEY_TILE = 256
TILE_UNROLL = 2
INT_MIN = -2147483648
INT_MAX = 2147483647
VMEM_LIMIT = 56 * 1024 * 1024


def _cparams(*sem):
    return pltpu.CompilerParams(dimension_semantics=sem, vmem_limit_bytes=VMEM_LIMIT)


def _full(shape):
    nd = len(shape)
    return pl.BlockSpec(shape, lambda *_: (0,) * nd)


def _round_up(n, m):
    return -(-n // m) * m


def _tile_lanes(t, reps):
    return t if reps == 1 else jnp.concatenate([t] * reps, axis=1)


def _rope(x, tab_ref, half):
    n = x.shape[1]
    reps = n // LANE
    cos = _tile_lanes(tab_ref[0], reps)
    sin_lo = _tile_lanes(tab_ref[1], reps)
    sin_hi = _tile_lanes(tab_ref[2], reps)
    return x * cos + pltpu.roll(x, n - half, 1) * sin_lo + pltpu.roll(x, half, 1) * sin_hi


def _mm(a, b):
    if b.dtype == F32:
        return jnp.dot(a.astype(F32), b, preferred_element_type=F32, precision=lax.Precision.HIGHEST)
    return jnp.dot(a.astype(BF16), b, preferred_element_type=F32)


def _mm_t(a, b):
    prec = lax.Precision.HIGHEST if b.dtype == F32 else None
    return lax.dot_general(a.astype(b.dtype), b, (((1,), (1,)), ((), ())), preferred_element_type=F32,
                           precision=prec)


def _pair_heads(parts):
    lo = parts[0] + pltpu.roll(parts[1], 64, 1)
    hi = parts[2] + pltpu.roll(parts[3], 64, 1)
    return jnp.concatenate([lo, hi], axis=1)


def _proj_in_kernel(x_ref, wa_ref, wb_ref, wc_ref, za_ref, zb_ref, zc_ref):
    xb = x_ref[...].astype(wa_ref.dtype)
    za_ref[...] = _mm(xb, wa_ref[...]).astype(za_ref.dtype)
    zb_ref[...] = _mm(xb, wb_ref[...]).astype(zb_ref.dtype)
    zc_ref[...] = _mm(xb, wc_ref[...]).astype(zc_ref.dtype)


def _proj_in(x2d, wa, wb, wc, tm):
    t = x2d.shape[0]
    row = lambda w: pl.BlockSpec((tm, w), lambda i: (i, 0))
    return pl.pallas_call(
        _proj_in_kernel, grid=(t // tm,),
        in_specs=[row(D_MODEL), _full(wa.shape), _full(wb.shape), _full(wc.shape)],
        out_specs=[row(A_W), row(B_W), row(C_W)],
        out_shape=[jax.ShapeDtypeStruct((t, w), wa.dtype) for w in (A_W, B_W, C_W)],
        compiler_params=_cparams("parallel"), name="proj_in")(x2d, wa, wb, wc)


def _rms(x, g):
    return x * lax.rsqrt(jnp.mean(x * x, axis=-1, keepdims=True) + RMS_EPS) * g


def _mla_prep_kernel(za_ref, tq_ref, tk_ref, qg_ref, kg_ref, wuq_ref, wk_ref, wv_ref,
                     q_ref, kc_ref, v_ref, ckv_ref, kr_ref):
    za = za_ref[...].astype(F32)
    qn = _rms(za[:, :Q_LORA], qg_ref[...])
    q = _rope(_mm(qn, wuq_ref[...]), tq_ref, ROPE_A // 2) * MLA_SCALE
    ckvn = _rms(za[:, Q_LORA:Q_LORA + KV_LORA], kg_ref[...])
    ckv_ref[...] = ckvn
    kr = _rope(za[:, Q_LORA + KV_LORA:], tk_ref, ROPE_A // 2)
    kr_ref[...] = kr[:, :ROPE_A]
    kc = _mm(ckvn, wk_ref[...]) + _tile_lanes(pltpu.roll(kr, NOPE_A, 1), H_A)
    v = _mm(ckvn, wv_ref[...]) + _ones_lane(H_A * HEAD_W)
    for h in range(H_A):
        sl = slice(h * HEAD_W, (h + 1) * HEAD_W)
        q_ref[h] = q[:, sl].astype(q_ref.dtype)
        kc_ref[h] = kc[:, sl].astype(kc_ref.dtype)
        v_ref[h] = v[:, sl].astype(v_ref.dtype)


def _mla_prep(za3, tq, tk, qg, kg, wuq, wk, wv, tl):
    b, l, _ = za3.shape
    heads = pl.BlockSpec((None, H_A, tl, HEAD_W), lambda bi, i: (bi, 0, i, 0))
    tab = pl.BlockSpec((3, tl, LANE), lambda bi, i: (0, i, 0))
    hshape = jax.ShapeDtypeStruct((b, H_A, l, HEAD_W), wuq.dtype)
    return pl.pallas_call(
        _mla_prep_kernel, grid=(b, l // tl),
        in_specs=[pl.BlockSpec((None, tl, A_W), lambda bi, i: (bi, i, 0)), tab, tab,
                  _full(qg.shape), _full(kg.shape), _full(wuq.shape), _full(wk.shape), _full(wv.shape)],
        out_specs=[heads, heads, heads,
                   pl.BlockSpec((None, tl, KV_LORA), lambda bi, i: (bi, i, 0)),
                   pl.BlockSpec((None, tl, ROPE_A), lambda bi, i: (bi, i, 0))],
        out_shape=[hshape, hshape, hshape,
                   jax.ShapeDtypeStruct((b, l, KV_LORA), F32), jax.ShapeDtypeStruct((b, l, ROPE_A), F32)],
        compiler_params=_cparams("parallel", "parallel"), name="mla_prep")(za3, tq, tk, qg, kg, wuq, wk, wv)


def _mla_past_kernel(ckv_ref, kr_ref, wk_ref, wv_ref, kc_ref, v_ref):
    cb = ckv_ref[...]
    kc = _mm(cb, wk_ref[...]) + _tile_lanes(pltpu.roll(kr_ref[...], NOPE_A, 1), H_A)
    v = _mm(cb, wv_ref[...]) + _ones_lane(H_A * HEAD_W)
    for h in range(H_A):
        sl = slice(h * HEAD_W, (h + 1) * HEAD_W)
        kc_ref[h] = kc[:, sl].astype(kc_ref.dtype)
        v_ref[h] = v[:, sl].astype(v_ref.dtype)


def _mla_past(ckv, kr128, wk, wv, tl):
    b, p, _ = ckv.shape
    heads = pl.BlockSpec((None, H_A, tl, HEAD_W), lambda bi, i: (bi, 0, i, 0))
    hshape = jax.ShapeDtypeStruct((b, H_A, p, HEAD_W), wk.dtype)
    return pl.pallas_call(
        _mla_past_kernel, grid=(b, p // tl),
        in_specs=[pl.BlockSpec((None, tl, KV_LORA), lambda bi, i: (bi, i, 0)),
                  pl.BlockSpec((None, tl, LANE), lambda bi, i: (bi, i, 0)), _full(wk.shape), _full(wv.shape)],
        out_specs=[heads, heads], out_shape=[hshape, hshape],
        compiler_params=_cparams("parallel", "parallel"), name="mla_past")(ckv, kr128, wk, wv)


def _ones_lane(width):
    lane = lax.broadcasted_iota(I32, (1, width), 1)
    return jnp.where((lane & (HEAD_W - 1)) == V_A, 1.0, 0.0)


def _fold_lanes_max(s):
    out = s[:, :LANE]
    for c0 in range(LANE, s.shape[1], LANE):
        out = jnp.maximum(out, s[:, c0:c0 + LANE])
    return out


def _tile_groups(nt, masked):
    if isinstance(nt, int):
        unroll = TILE_UNROLL if nt % TILE_UNROLL == 0 else 1
        return nt // unroll, unroll
    assert masked, "a rounded-up tile count needs a mask"
    return (nt + TILE_UNROLL - 1) // TILE_UNROLL, TILE_UNROLL


def _group_offsets(g, u, unroll, tk, last_tile):
    j = g * unroll + u
    off = pl.multiple_of(j * tk, tk)
    if unroll == 1:
        return off, off
    return off, pl.multiple_of(jnp.minimum(j, last_tile) * tk, tk)


def _attend(q_ref, sources, s_scr, tq, n_heads, mask_fn):
    heads = range(n_heads)
    qs = [q_ref[h] for h in heads]
    mrun = tuple(jnp.full((tq, LANE), NEG_INF, F32) for _ in heads)
    for (k_ref, _, base, tk, nt, masked) in sources:
        groups, unroll = _tile_groups(nt, masked)
        last = k_ref.shape[1] // tk - 1

        def score_body(g, ms, k_ref=k_ref, base=base, tk=tk, masked=masked, unroll=unroll, last=last):
            for u in range(unroll):
                off, kv_off = _group_offsets(g, u, unroll, tk, last)
                col = pl.multiple_of(base + off, tk)
                ss = [_mm_t(qs[h], k_ref[h, pl.ds(kv_off, tk), :]) for h in heads]
                if masked:
                    mask = mask_fn(off, col, tk)
                    ss = [jnp.where(mask, s, NEG_INF) for s in ss]
                for h in heads:
                    s_scr[h, :, pl.ds(col, tk)] = ss[h]
                if tk % LANE == 0:
                    ms = tuple(jnp.maximum(ms[h], _fold_lanes_max(ss[h])) for h in heads)
                else:
                    ms = tuple(jnp.maximum(ms[h], jnp.max(ss[h], axis=1, keepdims=True)) for h in heads)
            return ms
        mrun = lax.fori_loop(0, groups, score_body, mrun)
    m = [jnp.max(mr, axis=1, keepdims=True) for mr in mrun]
    accs = tuple(jnp.zeros((tq, HEAD_W), F32) for _ in heads)
    for (_, v_ref, base, tk, nt, masked) in sources:
        groups, unroll = _tile_groups(nt, masked)
        last = v_ref.shape[1] // tk - 1

        def pv_body(g, acc, v_ref=v_ref, base=base, tk=tk, unroll=unroll, last=last):
            for u in range(unroll):
                off, kv_off = _group_offsets(g, u, unroll, tk, last)
                col = pl.multiple_of(base + off, tk)
                ps = [jnp.exp(s_scr[h, :, pl.ds(col, tk)] - m[h]) for h in heads]
                acc = tuple(acc[h] + _mm(ps[h], v_ref[h, pl.ds(kv_off, tk), :]) for h in heads)
            return acc
        accs = lax.fori_loop(0, groups, pv_body, accs)
    lane = lax.broadcasted_iota(I32, (1, HEAD_W), 1)
    return [jnp.where(lane < V_A, a / a[:, V_A:V_A + 1], 0.0) for a in accs]


def _mla_attn_kernel(*refs, tq, tkn, ln, causal, first_key, p_len, tkp):
    if p_len:
        q_ref, kp_ref, vp_ref, kn_ref, vn_ref, o_ref, s_scr = refs
    else:
        q_ref, kn_ref, vn_ref, o_ref, s_scr = refs
    qt = pl.program_id(1)
    if causal:
        row = qt * tq + lax.broadcasted_iota(I32, (tq, 1), 0)
        nvis = ((row >> 6) + 1) << 6
        n_tiles = (qt * tq + tq + tkn - 1) // tkn
    else:
        nvis = None
        n_tiles = ln // tkn
    sources = []
    if p_len:
        sources.append((kp_ref, vp_ref, 0, tkp, p_len // tkp, False))
    sources.append((kn_ref, vn_ref, p_len, tkn, n_tiles, causal))

    def mask_fn(off, col, tk):
        key = off + lax.broadcasted_iota(I32, (1, tk), 1)
        return (key >= first_key) & (key < nvis)

    o_ref[...] = _pair_heads(_attend(q_ref, sources, s_scr, tq, H_A, mask_fn))


def _mla_attn(q, kn, vn, past, tq, causal, first_key):
    b, _, l, _ = q.shape
    ln = kn.shape[2]
    tkn = KEY_TILE if ln % KEY_TILE == 0 else ln
    qspec = pl.BlockSpec((None, H_A, tq, HEAD_W), lambda bi, i: (bi, 0, i, 0))
    whole = lambda n: pl.BlockSpec((None, H_A, n, HEAD_W), lambda bi, i: (bi, 0, 0, 0))
    args, specs, p_len = [q], [qspec], 0
    if past is not None:
        p_len = past[0].shape[2]
        args += list(past)
        specs += [whole(p_len), whole(p_len)]
    args += [kn, vn]
    specs += [whole(ln), whole(ln)]
    kern = functools.partial(_mla_attn_kernel, tq=tq, tkn=tkn, ln=ln, causal=causal, first_key=first_key,
                             p_len=p_len, tkp=KEY_TILE)
    return pl.pallas_call(
        kern, grid=(b, l // tq), in_specs=specs,
        out_specs=pl.BlockSpec((None, tq, H_A * V_A), lambda bi, i: (bi, i, 0)),
        out_shape=jax.ShapeDtypeStruct((b, l, H_A * V_A), F32),
        scratch_shapes=[pltpu.VMEM((H_A, tq, p_len + _round_up(ln, TILE_UNROLL * tkn)), F32)],
        compiler_params=_cparams("parallel", "arbitrary"), name="mla_attn")(*args)


def _dsa_prep_kernel(zb_ref, tqk_ref, tiq_ref, tik_ref,
                     q_ref, k_ref, v_ref, qi_ref, ki_ref, wi_ref, kst_ref, vst_ref, kist_ref):
    zb = zb_ref[...].astype(F32)
    q = _rope(zb[:, B_Q:B_K], tqk_ref, DH_B // 2) * DSA_SCALE
    k = _rope(zb[:, B_K:B_V], tqk_ref, DH_B // 2)
    v = zb[:, B_V:B_QI]
    qi = _rope(zb[:, B_QI:B_KI], tiq_ref, D_I // 2)
    kiw = _rope(zb[:, B_KI:B_W], tik_ref, D_I // 2)
    ks, vs = [], []
    for h in range(H_B):
        sl = slice(h * HEAD_W, (h + 1) * HEAD_W)
        q_ref[h] = q[:, sl].astype(q_ref.dtype)
        k_ref[h] = k[:, sl].astype(k_ref.dtype)
        v_ref[h] = (v[:, sl] + _ones_lane(HEAD_W)).astype(v_ref.dtype)
        ks.append(k[:, sl])
        vs.append(v[:, sl])
    for h in range(H_I):
        qi_ref[h] = qi[:, h * D_I:(h + 1) * D_I].astype(qi_ref.dtype)
    ki_ref[...] = kiw[:, :D_I].astype(ki_ref.dtype)
    wi_ref[...] = kiw * IDX_SCALE
    kst_ref[...] = _pair_heads(ks)
    vst_ref[...] = _pair_heads(vs)
    kist_ref[...] = kiw[:, :D_I]


def _dsa_prep(zb3, tqk, tiq, tik, tl, mm_dtype):
    b, l, _ = zb3.shape
    heads = pl.BlockSpec((None, H_B, tl, HEAD_W), lambda bi, i: (bi, 0, i, 0))
    tab = pl.BlockSpec((3, tl, LANE), lambda bi, i: (0, i, 0))
    rows = lambda w: pl.BlockSpec((None, tl, w), lambda bi, i: (bi, i, 0))
    hshape = jax.ShapeDtypeStruct((b, H_B, l, HEAD_W), mm_dtype)
    return pl.pallas_call(
        _dsa_prep_kernel, grid=(b, l // tl),
        in_specs=[rows(B_W), tab, tab, tab],
        out_specs=[heads, heads, heads,
                   pl.BlockSpec((None, H_I, tl, D_I), lambda bi, i: (bi, 0, i, 0)),
                   rows(D_I), rows(LANE), rows(H_B * DH_B), rows(H_B * DH_B), rows(D_I)],
        out_shape=[hshape, hshape, hshape,
                   jax.ShapeDtypeStruct((b, H_I, l, D_I), mm_dtype),
                   jax.ShapeDtypeStruct((b, l, D_I), mm_dtype),
                   jax.ShapeDtypeStruct((b, l, LANE), F32),
                   jax.ShapeDtypeStruct((b, l, H_B * DH_B), F32),
                   jax.ShapeDtypeStruct((b, l, H_B * DH_B), F32),
                   jax.ShapeDtypeStruct((b, l, D_I), F32)],
        compiler_params=_cparams("parallel", "parallel"), name="dsa_prep")(zb3, tqk, tiq, tik)


def _order_key(x):
    x = jnp.where(x == 0.0, 0.0, x)
    b = pltpu.bitcast(x, I32)
    return jnp.where(b < 0, b ^ INT_MAX, b)


def _dsa_attn_kernel(*refs, tq, tkn, ln, causal, first_key, p_len, tkp, k_sel):
    if p_len:
        (q_ref, qi_ref, wi_ref, kp_ref, vp_ref, kip_ref, kn_ref, vn_ref, kin_ref, o_ref,
         keys_ref, s_scr, wrep_scr) = refs
    else:
        (q_ref, qi_ref, wi_ref, kn_ref, vn_ref, kin_ref, o_ref, keys_ref, s_scr, wrep_scr) = refs
    qt = pl.program_id(1)
    p_tiles = p_len // tkp if p_len else 0
    if causal:
        row = qt * tq + lax.broadcasted_iota(I32, (tq, 1), 0)
        nvis = ((row >> 6) + 1) << 6
        n_tiles = (qt * tq + tq + tkn - 1) // tkn
        n_valid = qt * tq + tq - first_key + p_len
    else:
        nvis = ln
        n_tiles = ln // tkn
        n_valid = ln + p_len

    sources = []
    if p_len:
        sources.append((kp_ref, vp_ref, kip_ref, 0, tkp, p_tiles, False))
    sources.append((kn_ref, vn_ref, kin_ref, p_len, tkn, n_tiles, causal))

    def col_mask(off, tk):
        col = off + lax.broadcasted_iota(I32, (1, tk), 1)
        return (col >= first_key) & (col < nvis)

    wi = wi_ref[...]
    qis = [qi_ref[h] for h in range(H_I)]
    for h in range(H_I):
        wrep_scr[h] = jnp.broadcast_to(wi[:, D_I + h:D_I + h + 1], (tq, LANE))
    for (_, _, ki_ref, base, tk, nt, masked) in sources:
        groups, unroll = _tile_groups(nt, masked)
        last = ki_ref.shape[0] // tk - 1

        def score_body(g, c, ki_ref=ki_ref, base=base, tk=tk, masked=masked, unroll=unroll, last=last):
            wts = [_tile_lanes(wrep_scr[h], tk // LANE) if tk >= LANE else wrep_scr[h][:, :tk] for h in range(H_I)]
            for u in range(unroll):
                off, kv_off = _group_offsets(g, u, unroll, tk, last)
                ki_t = ki_ref[pl.ds(kv_off, tk), :]
                sc = jnp.maximum(_mm_t(qis[0], ki_t), 0.0) * wts[0]
                for h in range(1, H_I):
                    sc = sc + jnp.maximum(_mm_t(qis[h], ki_t), 0.0) * wts[h]
                key = _order_key(sc)
                if masked:
                    key = jnp.where(col_mask(off, tk), key, INT_MIN)
                keys_ref[:, pl.ds(pl.multiple_of(base + off, tk), tk)] = key
            return c
        lax.fori_loop(0, groups, score_body, 0)

    count_sources = []
    for (_, _, _, base, tk, nt, _) in sources:
        if tk == KEY_TILE:
            count_sources.append((base, TILE_UNROLL * tk, _tile_groups(nt, True)[0] if not isinstance(nt, int)
                                  else nt // TILE_UNROLL))
        else:
            count_sources.append((base, tk, nt))

    def count_ge(mid):
        total = jnp.zeros((tq, 1), F32)
        for (base, tk, nt) in count_sources:
            wacc = min(tk, LANE)

            def cnt_body(j, acc, base=base, tk=tk, wacc=wacc):
                off = pl.multiple_of(base + j * tk, tk)
                hit = jnp.where(keys_ref[:, pl.ds(off, tk)] >= mid, 1.0, 0.0)
                for c0 in range(0, tk, wacc):
                    acc = acc + hit[:, c0:c0 + wacc]
                return acc
            acc = lax.fori_loop(0, nt, cnt_body, jnp.zeros((tq, wacc), F32))
            total = total + jnp.sum(acc, axis=1, keepdims=True)
        return total

    def bisect_body(_, c):
        lo, hi, n_lo = c
        mid = (lo >> 1) + (hi >> 1) + (((lo & 1) + (hi & 1) + 1) >> 1)
        n_mid = count_ge(mid)
        ge = n_mid >= float(k_sel)
        return jnp.where(ge, mid, lo), jnp.where(ge, hi, mid - 1), jnp.where(ge, n_mid, n_lo)

    n_iter = jnp.where(n_valid > k_sel, 32, 0)
    thr, _, n_sel = lax.fori_loop(0, n_iter, bisect_body,
                                  (jnp.full((tq, 1), INT_MIN + 1, I32), jnp.full((tq, 1), INT_MAX, I32),
                                   jnp.zeros((tq, 1), F32)))

    has_ties = jnp.max(jnp.where(n_sel > float(k_sel), 1.0, 0.0)) > 0.0

    @pl.when(has_ties)
    def _():
        keep = float(k_sel) - count_ge(thr + 1)
        run = jnp.zeros((tq, 1), F32)
        for (_, _, _, base, tk, nt, _) in sources:
            upper = jnp.where(lax.broadcasted_iota(I32, (tk, tk), 0) <= lax.broadcasted_iota(I32, (tk, tk), 1),
                              1.0, 0.0).astype(BF16)

            def tie_body(j, run, base=base, tk=tk, upper=upper):
                col = pl.multiple_of(base + j * tk, tk)
                kt = keys_ref[:, pl.ds(col, tk)]
                eq = kt == thr
                eqf = jnp.where(eq, 1.0, 0.0)
                rank = run + _mm(eqf, upper)
                keys_ref[:, pl.ds(col, tk)] = jnp.where(eq & (rank > keep), INT_MIN, kt)
                return run + jnp.sum(eqf, axis=1, keepdims=True)
            run = lax.fori_loop(0, nt, tie_body, run)

    def selected(off, col, tk):
        return keys_ref[:, pl.ds(col, tk)] >= thr

    att_sources = [(k_ref, v_ref, base, tk, nt, True) for (k_ref, v_ref, _, base, tk, nt, _) in sources]
    o_ref[...] = _pair_heads(_attend(q_ref, att_sources, s_scr, tq, H_B, selected))


def _dsa_attn(q, qi, wi, kn, vn, kin, past, tq, causal, first_key, k_sel):
    b, _, l, _ = q.shape
    ln = kn.shape[2]
    tkn = KEY_TILE if ln % KEY_TILE == 0 else ln
    qspec = pl.BlockSpec((None, H_B, tq, HEAD_W), lambda bi, i: (bi, 0, i, 0))
    whole = lambda n: pl.BlockSpec((None, H_B, n, HEAD_W), lambda bi, i: (bi, 0, 0, 0))
    whole_ki = lambda n: pl.BlockSpec((None, n, D_I), lambda bi, i: (bi, 0, 0))
    args = [q, qi, wi]
    specs = [qspec, pl.BlockSpec((None, H_I, tq, D_I), lambda bi, i: (bi, 0, i, 0)),
             pl.BlockSpec((None, tq, LANE), lambda bi, i: (bi, i, 0))]
    p_len = 0
    if past is not None:
        p_len = past[0].shape[2]
        args += list(past)
        specs += [whole(p_len), whole(p_len), whole_ki(p_len)]
    args += [kn, vn, kin]
    specs += [whole(ln), whole(ln), whole_ki(ln)]
    kern = functools.partial(_dsa_attn_kernel, tq=tq, tkn=tkn, ln=ln, causal=causal, first_key=first_key,
                             p_len=p_len, tkp=KEY_TILE, k_sel=k_sel)
    return pl.pallas_call(
        kern, grid=(b, l // tq), in_specs=specs,
        out_specs=pl.BlockSpec((None, tq, H_B * DH_B), lambda bi, i: (bi, i, 0)),
        out_shape=jax.ShapeDtypeStruct((b, l, H_B * DH_B), F32),
        scratch_shapes=[pltpu.VMEM((tq, p_len + _round_up(ln, TILE_UNROLL * tkn)), I32),
                        pltpu.VMEM((H_B, tq, p_len + _round_up(ln, TILE_UNROLL * tkn)), F32),
                        pltpu.VMEM((H_I, tq, LANE), F32)],
        compiler_params=_cparams("parallel", "arbitrary"), name="dsa_attn")(*args)


def _softplus(x):
    return jnp.maximum(x, 0.0) + jnp.log(1.0 + jnp.exp(-jnp.abs(x)))


def _rwkv_prep_kernel(zc_ref, zp_ref, sh_ref, mu_ref, w0_ref, a0_ref, kkw_ref, ka_ref,
                      wup_ref, aup_ref, gup_ref, bd_ref,
                      r_ref, k_ref, v_ref, ld_ref, kk_ref, b_ref, g_ref, *, tl, lo, hi):
    i = pl.program_id(1)
    z = zc_ref[...].astype(F32)
    rowi = lax.broadcasted_iota(I32, (tl, 1), 0)
    last = zp_ref.shape[0] - 1
    prev_last = jnp.where(i == 0, sh_ref[...], zp_ref[last:last + 1, :].astype(F32))
    zprev = jnp.where(rowi == 0, prev_last, pltpu.roll(z, 1, 0))
    zs = z + (zprev - z) * mu_ref[...]
    r, k, v, lora = zs[:, :W_C], zs[:, W_C:2 * W_C], zs[:, 2 * W_C:3 * W_C], zs[:, 3 * W_C:]
    lane = lax.broadcasted_iota(I32, (1, LANE), 1)
    u = jnp.where(lane < W_LORA, jnp.tanh(lora),
                  jnp.where(lane < W_LORA + A_LORA, lora, jax.nn.sigmoid(lora)))
    w = -_softplus(-(w0_ref[...] + _mm(u, wup_ref[...]))) - 0.5
    ld = -jnp.exp(w)
    a = jax.nn.sigmoid(a0_ref[...] + _mm(u, aup_ref[...]))
    g = _mm(u, gup_ref[...])
    kmod = k * (1.0 + (a - 1.0) * ka_ref[...])
    kk = k * kkw_ref[...]
    sq = kk * kk
    bd = bd_ref[...]
    if bd.dtype == F32:
        ss = _mm(sq, bd)
    else:
        sq_hi = sq.astype(BF16)
        ss = _mm(sq_hi, bd) + _mm(sq - sq_hi.astype(F32), bd)
    kkn = kk * jnp.minimum(lax.rsqrt(ss), 1e12)
    grow = i * tl + rowi
    valid = ((grow >= lo) & (grow < hi)).astype(F32)
    r_ref[...] = r.astype(r_ref.dtype)
    k_ref[...] = (kmod * valid).astype(k_ref.dtype)
    v_ref[...] = (v * valid).astype(v_ref.dtype)
    ld_ref[...] = ld * valid
    kk_ref[...] = (kkn * valid).astype(kk_ref.dtype)
    b_ref[...] = (kkn * a * valid).astype(b_ref.dtype)
    g_ref[...] = g.astype(g_ref.dtype)


def _rwkv_prep(zc3, shift, p, tl, lo, hi):
    b, l, _ = zc3.shape
    rows = lambda w: pl.BlockSpec((None, tl, w), lambda bi, i: (bi, i, 0))
    vec = _full((1, W_C))
    mat = _full((LANE, W_C))
    kern = functools.partial(_rwkv_prep_kernel, tl=tl, lo=lo, hi=hi)
    zdt = zc3.dtype
    pr = 8 if zdt == F32 else 16
    out = lambda dt: jax.ShapeDtypeStruct((b, l, W_C), dt)
    return pl.pallas_call(
        kern, grid=(b, l // tl),
        in_specs=[rows(C_W),
                  pl.BlockSpec((None, pr, C_W), lambda bi, i: (bi, jnp.maximum(i * (tl // pr) - 1, 0), 0)),
                  pl.BlockSpec((None, 1, C_W), lambda bi, i: (bi, 0, 0)),
                  _full((1, C_W)), vec, vec, vec, vec, mat, mat, mat, _full((W_C, W_C))],
        out_specs=[rows(W_C)] * 7,
        out_shape=[out(zdt), out(zdt), out(zdt), out(F32), out(zdt), out(zdt), out(zdt)],
        compiler_params=_cparams("parallel", "parallel"), name="rwkv_prep")(
            zc3, zc3, shift, p['mu'], p['w0'], p['a0'], p['k_k'], p['k_a'], p['wup'], p['aup'], p['gup'], p['bd'])


def _rwkv_scan_kernel(r_ref, k_ref, v_ref, ld_ref, kk_ref, b_ref, g_ref, s0_ref, lng_ref, lnb_ref, rk_ref,
                      o_ref, sout_ref, s_scr, *, c, n_chunks, md, nb):
    ci = pl.program_id(1)

    @pl.when(ci == 0)
    def _():
        s_scr[...] = s0_ref[...]

    rowi = lax.broadcasted_iota(I32, (c, 1), 0)
    ri = lax.broadcasted_iota(I32, (c, c), 0)
    cj = lax.broadcasted_iota(I32, (c, c), 1)
    strict, incl = ri > cj, ri >= cj
    eye = jnp.where(ri == cj, 1.0, 0.0)
    n_sq = c.bit_length() - 2

    rows = []
    for bi in range(nb):
        ld = ld_ref[bi]
        cum = ld
        step = 1
        while step < c:
            cum = cum + jnp.where(rowi >= step, pltpu.roll(cum, step, 0), 0.0)
            step *= 2
        cl = cum[c - 1:c, :]
        e_in, e_ex, e_neg, e_end = jnp.exp(cum), jnp.exp(cum - ld), jnp.exp(-cum), jnp.exp(cl - cum)
        r, k, v, kk, bb = (x[bi].astype(F32) for x in (r_ref, k_ref, v_ref, kk_ref, b_ref))
        rows.append(dict(d_end=jnp.exp(cl), v=v, g=g_ref[bi].astype(F32),
                         rt=r * e_in, kap=kk * e_ex, kt=k * e_neg, bt=bb * e_neg,
                         kte=k * e_end, bte=bb * e_end, bonus_rk=r * k * rk_ref[...]))

    units = [(bi, h) for bi in range(nb) for h in range(H_C)]
    idx = range(len(units))
    sls = [slice(h * N_C, (h + 1) * N_C) for _, h in units]
    s_old = [s_scr[bi, h] for bi, h in units]
    s_mm = [s.astype(md) for s in s_old]
    left = [jnp.concatenate([rows[bi]['kap'][:, sls[u]], rows[bi]['rt'][:, sls[u]]], axis=0).astype(md)
            for u, (bi, _) in enumerate(units)]
    right = [jnp.concatenate([rows[bi]['bt'][:, sls[u]], rows[bi]['kt'][:, sls[u]]], axis=0).astype(md)
             for u, (bi, _) in enumerate(units)]
    gram = [_mm_t(left[u], right[u]) for u in idx]
    a_kb = [jnp.where(strict, g[:c, :c], 0.0) for g in gram]
    a_kk = [jnp.where(strict, g[:c, c:], 0.0) for g in gram]
    a_rb = [jnp.where(incl, g[c:, :c], 0.0) for g in gram]
    a_rk = [jnp.where(incl, g[c:, c:], 0.0) for g in gram]
    vh = [rows[bi]['v'][:, sls[u]] for u, (bi, _) in enumerate(units)]
    vb = [x.astype(md) for x in vh]
    x0 = [_mm_t(left[u][:c], s_mm[u]) + _mm(a_kk[u], vb[u]) for u in idx]
    y0 = [_mm_t(left[u][c:], s_mm[u]) + _mm(a_rk[u], vb[u]) for u in idx]
    tinv = [eye - n for n in a_kb]
    npow = a_kb
    for _ in range(n_sq):
        npow_mm = [n.astype(md) for n in npow]
        npow = [_mm(n, n) for n in npow_mm]
        tinv = [tinv[u] + _mm(tinv[u], npow[u].astype(md)) for u in idx]
    w = [_mm(tinv[u], x0[u].astype(md)) for u in idx]
    wb = [x.astype(md) for x in w]
    y = [y0[u] - _mm(a_rb[u], wb[u]) for u in idx]
    for u, (bi, h) in enumerate(units):
        sl = sls[u]
        s_scr[bi, h] = (s_old[u] * rows[bi]['d_end'][:, sl] + _mm(vh[u].T, rows[bi]['kte'][:, sl].astype(md))
                        - _mm(w[u].T, rows[bi]['bte'][:, sl].astype(md)))
    for u, (bi, h) in enumerate(units):
        sl = sls[u]
        ym = jnp.mean(y[u], axis=-1, keepdims=True)
        yc = y[u] - ym
        yn = yc * lax.rsqrt(jnp.mean(yc * yc, axis=-1, keepdims=True) + GN_EPS) * lng_ref[:, sl] + lnb_ref[:, sl]
        bonus = jnp.sum(rows[bi]['bonus_rk'][:, sl], axis=-1, keepdims=True) * vh[u]
        o_ref[bi, :, sl] = (yn + bonus) * rows[bi]['g'][:, sl]

    @pl.when(ci == n_chunks - 1)
    def _():
        sout_ref[...] = s_scr[...]


def _rwkv_scan(pre, s0, p, c, mm_dtype):
    r = pre[0]
    b, l, _ = r.shape
    n_chunks = l // c
    nb = 2 if b % 2 == 0 else 1
    rows = pl.BlockSpec((nb, c, W_C), lambda bi, i: (bi, i, 0))
    state = pl.BlockSpec((nb, H_C, N_C, N_C), lambda bi, i: (bi, 0, 0, 0))
    vec = _full((1, W_C))
    kern = functools.partial(_rwkv_scan_kernel, c=c, n_chunks=n_chunks, md=mm_dtype, nb=nb)
    return pl.pallas_call(
        kern, grid=(b // nb, n_chunks),
        in_specs=[rows] * 7 + [state, vec, vec, vec],
        out_specs=[rows, state],
        out_shape=[jax.ShapeDtypeStruct((b, l, W_C), F32), jax.ShapeDtypeStruct((b, H_C, N_C, N_C), F32)],
        scratch_shapes=[pltpu.VMEM((nb, H_C, N_C, N_C), F32)],
        compiler_params=_cparams("parallel", "arbitrary"), name="rwkv_scan")(
            *pre, s0, p['ln_g'], p['ln_b'], p['r_k'])


def _layernorm(y, g, b):
    mu = jnp.mean(y, axis=-1, keepdims=True)
    yc = y - mu
    return yc * lax.rsqrt(jnp.mean(yc * yc, axis=-1, keepdims=True) + LN_EPS) * g + b


def _row_valid(i, tm, tiles_per_batch, lo, hi):
    rowb = (i % tiles_per_batch) * tm + lax.broadcasted_iota(I32, (tm, 1), 0)
    return (rowb >= lo) & (rowb < hi)


def _outproj_kernel(oa_ref, ob_ref, oc_ref, x_ref, wa_ref, wb_ref, wc_ref, g_ref, b_ref, y_ref,
                    *, tm, tiles_per_batch, lo, hi, masked):
    mix = _mm(oa_ref[...], wa_ref[...]) + _mm(ob_ref[...], wb_ref[...]) + _mm(oc_ref[...], wc_ref[...])
    y = _layernorm(ALPHA * x_ref[...] + mix, g_ref[...], b_ref[...])
    if masked:
        y = jnp.where(_row_valid(pl.program_id(0), tm, tiles_per_batch, lo, hi), y, 0.0)
    y_ref[...] = y


def _outproj(oa, ob, oc, x2d, p, tm, rows_per_batch, lo, hi, masked):
    t = x2d.shape[0]
    row = lambda w: pl.BlockSpec((tm, w), lambda i: (i, 0))
    kern = functools.partial(_outproj_kernel, tm=tm, tiles_per_batch=rows_per_batch // tm, lo=lo, hi=hi,
                             masked=masked)
    return pl.pallas_call(
        kern, grid=(t // tm,),
        in_specs=[row(oa.shape[1]), row(ob.shape[1]), row(oc.shape[1]), row(D_MODEL),
                  _full(p['wo_a'].shape), _full(p['wo_b'].shape), _full(p['wo_c'].shape),
                  _full((1, D_MODEL)), _full((1, D_MODEL))],
        out_specs=row(D_MODEL), out_shape=jax.ShapeDtypeStruct((t, D_MODEL), F32),
        compiler_params=_cparams("parallel"), name="outproj_ln")(
            oa, ob, oc, x2d, p['wo_a'], p['wo_b'], p['wo_c'], p['ln1_g'], p['ln1_b'])


def _router_kernel(x_ref, whi_ref, wlo_ref, br_ref, dg_ref):
    x = x_ref[...]
    xh = x.astype(BF16)
    xl = (x - xh.astype(F32)).astype(BF16)
    logits = _mm(xh, whi_ref[...]) + _mm(xl, whi_ref[...]) + _mm(xh, wlo_ref[...]) + br_ref[...]
    lane = lax.broadcasted_iota(I32, (1, LANE), 1).astype(F32)
    big = float(LANE)
    gm = (lane >= N_EXPERTS) & (lane < N_EXPERTS + N_GROUPS)
    gl = jnp.where(gm, logits, -jnp.inf)
    ge = jnp.exp(gl - jnp.max(gl, axis=1, keepdims=True))
    gp = ge / jnp.sum(ge, axis=1, keepdims=True)
    gval = jnp.max(gp, axis=1, keepdims=True)
    gidx = jnp.min(jnp.where(gm & (gp == gval), lane, big), axis=1, keepdims=True) - N_EXPERTS
    elo = gidx * EXP_PER_GROUP
    em = (lane >= elo) & (lane < elo + EXP_PER_GROUP)
    el = jnp.where(em, logits, -jnp.inf)
    ee = jnp.exp(el - jnp.max(el, axis=1, keepdims=True))
    ep = jnp.where(em, ee / jnp.sum(ee, axis=1, keepdims=True), -1.0)
    p1 = jnp.max(ep, axis=1, keepdims=True)
    i1 = jnp.min(jnp.where(ep == p1, lane, big), axis=1, keepdims=True)
    ep2 = jnp.where(lane == i1, -1.0, ep)
    p2 = jnp.max(ep2, axis=1, keepdims=True)
    i2 = jnp.min(jnp.where((ep2 == p2) & (lane != i1), lane, big), axis=1, keepdims=True)
    den = p1 + p2
    dg_ref[...] = jnp.where(lane == i1, gval * p1 / den, 0.0) + jnp.where(lane == i2, gval * p2 / den, 0.0)


def _router(x2d, p, tm):
    t = x2d.shape[0]
    row = lambda w: pl.BlockSpec((tm, w), lambda i: (i, 0))
    return pl.pallas_call(
        _router_kernel, grid=(t // tm,),
        in_specs=[row(D_MODEL), _full((D_MODEL, LANE)), _full((D_MODEL, LANE)), _full((1, LANE))],
        out_specs=row(LANE), out_shape=jax.ShapeDtypeStruct((t, LANE), F32),
        compiler_params=_cparams("parallel"), name="router")(x2d, p['wr_hi'], p['wr_lo'], p['br'])


def _moe_kernel(x_ref, dg_ref, wgu_ref, wd_ref, g_ref, b_ref, y_ref, acc_ref, xb_ref,
                *, tm, tiles_per_batch, lo, hi, masked, fused_down):
    e = pl.program_id(1)

    @pl.when(e == 0)
    def _():
        xb_ref[...] = x_ref[...].astype(xb_ref.dtype)
        if not fused_down:
            acc_ref[...] = jnp.zeros_like(acc_ref)

    hcat = _mm(xb_ref[...], wgu_ref[...])
    hg, hu = hcat[:, :D_EXPERT], hcat[:, D_EXPERT:]
    lane = lax.broadcasted_iota(I32, (1, LANE), 1)
    gate = jnp.sum(jnp.where(lane == e, dg_ref[...], 0.0), axis=1, keepdims=True)
    hidden = hg * jax.nn.sigmoid(hg) * hu * gate
    if fused_down:
        acc_ref[:, pl.ds(pl.multiple_of(e * D_EXPERT, D_EXPERT), D_EXPERT)] = hidden.astype(acc_ref.dtype)
    else:
        acc_ref[...] += _mm(hidden, wd_ref[...])

    @pl.when(e == N_EXPERTS - 1)
    def _():
        ffn = _mm(acc_ref[...], wd_ref[...]) if fused_down else acc_ref[...]
        y = _layernorm(ALPHA * x_ref[...] + ffn, g_ref[...], b_ref[...])
        if masked:
            y = jnp.where(_row_valid(pl.program_id(0), tm, tiles_per_batch, lo, hi), y, 0.0)
        y_ref[...] = y


def _moe(x2d, dg, p, tm, rows_per_batch, lo, hi, masked):
    t = x2d.shape[0]
    row = lambda w: pl.BlockSpec((tm, w), lambda i, e: (i, 0))
    md = p['wgu'].dtype
    fused_down = md == BF16
    kern = functools.partial(_moe_kernel, tm=tm, tiles_per_batch=rows_per_batch // tm, lo=lo, hi=hi,
                             masked=masked, fused_down=fused_down)
    if fused_down:
        wd = p['wd'].reshape(N_EXPERTS * D_EXPERT, D_MODEL)
        wd_spec = pl.BlockSpec(wd.shape, lambda i, e: (0, 0), pipeline_mode=pl.Buffered(1))
        acc = pltpu.VMEM((tm, N_EXPERTS * D_EXPERT), md)
    else:
        wd = p['wd']
        wd_spec = pl.BlockSpec((None, D_EXPERT, D_MODEL), lambda i, e: (e, 0, 0))
        acc = pltpu.VMEM((tm, D_MODEL), F32)
    return pl.pallas_call(
        kern, grid=(t // tm, N_EXPERTS),
        in_specs=[row(D_MODEL), row(LANE),
                  pl.BlockSpec((None, D_MODEL, 2 * D_EXPERT), lambda i, e: (e, 0, 0)), wd_spec,
                  pl.BlockSpec((1, D_MODEL), lambda i, e: (0, 0)), pl.BlockSpec((1, D_MODEL), lambda i, e: (0, 0))],
        out_specs=row(D_MODEL), out_shape=jax.ShapeDtypeStruct((t, D_MODEL), F32),
        scratch_shapes=[acc, pltpu.VMEM((tm, D_MODEL), md)],
        compiler_params=_cparams("parallel", "arbitrary"), name="moe_ln")(
            x2d, dg, p['wgu'], wd, p['ln2_g'], p['ln2_b'])


def _rope_table(pos, pattern):
    posf = pos.astype(F32)[:, None]
    n = pos.shape[0]
    cos, s_lo, s_hi = [], [], []
    for kind, w in pattern:
        if kind == 'rope':
            half = w // 2
            inv = ROPE_THETA ** (-jnp.arange(half, dtype=F32) / half)
            ang = posf * inv[None, :]
            c, s, z = jnp.cos(ang), jnp.sin(ang), jnp.zeros((n, half), F32)
            cos += [c, c]
            s_lo += [-s, z]
            s_hi += [z, s]
        else:
            fill = jnp.full((n, w), 1.0 if kind == 'one' else 0.0, F32)
            z = jnp.zeros((n, w), F32)
            cos.append(fill)
            s_lo.append(z)
            s_hi.append(z)
    return jnp.stack([jnp.concatenate(t, axis=1) for t in (cos, s_lo, s_hi)])


def _tables(pos):
    return {
        'mla_q': _rope_table(pos, [('one', NOPE_A), ('rope', ROPE_A), ('zero', HEAD_W - NOPE_A - ROPE_A)]),
        'mla_kr': _rope_table(pos, [('rope', ROPE_A), ('zero', LANE - ROPE_A)]),
        'dsa_qk': _rope_table(pos, [('rope', DH_B), ('zero', HEAD_W - DH_B)]),
        'idx_q': _rope_table(pos, [('rope', D_I)] * (LANE // D_I)),
        'idx_k': _rope_table(pos, [('rope', D_I), ('one', H_I), ('zero', LANE - D_I - H_I)]),
    }


def _pad_cols(w, width):
    return jnp.pad(w, [(0, 0)] * (w.ndim - 1) + [(0, width - w.shape[-1])])


def _head_pad(w, n_heads, width):
    d = w.shape[-1] // n_heads
    w = w.reshape(w.shape[:-1] + (n_heads, d))
    return _pad_cols(w, width).reshape(w.shape[:-2] + (n_heads * width,))


def _prepare_weights(w, md):
    w_in = w['w_in']
    a, bseg, cseg = w_in[..., :A_COLS], w_in[..., A_COLS:A_COLS + B_COLS], w_in[..., A_COLS + B_COLS:]
    hd = H_B * DH_B
    wb = jnp.concatenate([
        _head_pad(bseg[..., 0:hd], H_B, HEAD_W), _head_pad(bseg[..., hd:2 * hd], H_B, HEAD_W),
        _head_pad(bseg[..., 2 * hd:3 * hd], H_B, HEAD_W), bseg[..., 3 * hd:3 * hd + H_I * D_I],
        _pad_cols(bseg[..., 3 * hd + H_I * D_I:], LANE)], axis=-1)
    w_uq = w['mla_w_uq'].reshape(DEPTH, Q_LORA, H_A, NOPE_A + ROPE_A)
    w_ukv = w['mla_w_ukv'].reshape(DEPTH, KV_LORA, H_A, NOPE_A + V_A)
    zeros_lora = lambda n: jnp.zeros((DEPTH, n, W_C), F32)
    head_of = jnp.arange(W_C) // N_C
    wr = jnp.concatenate([w['router_expert'], w['router_group']], axis=-1)
    wr = _pad_cols(wr, LANE)
    wr_hi = wr.astype(BF16)
    row = lambda v: v.reshape(DEPTH, 1, -1)
    return {
        'wa': _pad_cols(a, A_W).astype(md), 'wb': wb.astype(md), 'wc': cseg.astype(md),
        'qg': row(w['mla_q_norm']), 'kg': row(w['mla_kv_norm']),
        'wuq': _pad_cols(w_uq, HEAD_W).reshape(DEPTH, Q_LORA, H_A * HEAD_W).astype(md),
        'wk': _pad_cols(w_ukv[..., :NOPE_A], HEAD_W).reshape(DEPTH, KV_LORA, H_A * HEAD_W).astype(md),
        'wv': _pad_cols(w_ukv[..., NOPE_A:], HEAD_W).reshape(DEPTH, KV_LORA, H_A * HEAD_W).astype(md),
        'mu': row(w['rwkv_mu']), 'w0': row(w['rwkv_w0']), 'a0': row(w['rwkv_a0']),
        'k_k': row(w['rwkv_k_k']), 'k_a': row(w['rwkv_k_a']), 'r_k': row(w['rwkv_r_k']),
        'ln_g': row(w['rwkv_ln_g']), 'ln_b': row(w['rwkv_ln_b']),
        'wup': jnp.concatenate([w['rwkv_w_up'], zeros_lora(LANE - W_LORA)], axis=1).astype(md),
        'aup': jnp.concatenate([zeros_lora(W_LORA), w['rwkv_a_up'], zeros_lora(G_LORA)], axis=1).astype(md),
        'gup': jnp.concatenate([zeros_lora(W_LORA + A_LORA), w['rwkv_g_up']], axis=1).astype(md),
        'bd': jnp.broadcast_to((head_of[:, None] == head_of[None, :]).astype(md), (DEPTH, W_C, W_C)),
        'wo_a': w['w_out'][:, :H_A * V_A].astype(md),
        'wo_b': w['w_out'][:, H_A * V_A:H_A * V_A + hd].astype(md),
        'wo_c': w['w_out'][:, H_A * V_A + hd:].astype(md),
        'ln1_g': row(w['ln1_g']), 'ln1_b': row(w['ln1_b']), 'ln2_g': row(w['ln2_g']), 'ln2_b': row(w['ln2_b']),
        'wr_hi': wr_hi, 'wr_lo': (wr - wr_hi.astype(F32)).astype(BF16),
        'br': _pad_cols(jnp.concatenate([w['router_expert_b'], w['router_group_b']], axis=-1), LANE).reshape(DEPTH, 1, LANE),
        'wgu': jnp.concatenate([w['exp_w_gate'], w['exp_w_up']], axis=-1).astype(md),
        'wd': w['exp_w_down'].astype(md),
    }


def _largest_tile(n, limit):
    best = 8
    for t in range(8, min(n, limit) + 1, 8):
        if n % t == 0:
            best = t
    return best


def _trunk(x, pos, past, wp, *, lo, hi, causal, k_sel):
    b, l, _ = x.shape
    t = b * l
    tabs = _tables(pos)
    tl = min(256, l)
    tq_mla = min(4 * CHUNK, l)
    tq_dsa = min(2 * CHUNK, l)
    c_rwkv = min(CHUNK, l)
    masked = not (lo == 0 and hi == l)
    md = wp['wa'].dtype
    tm_cap = 512 if md == BF16 else 128
    tm = _largest_tile(l, tm_cap) if masked else _largest_tile(t, tm_cap)
    tm_moe = _largest_tile(l, 1100) if masked else _largest_tile(t, 1100)
    first_key = lo if causal else 0
    new = {k: [] for k in ('ckv', 'krope', 'dsa_k', 'dsa_v', 'dsa_kidx', 'rwkv', 'shift')}
    x2d = x.reshape(t, D_MODEL)
    for layer in range(DEPTH):
        p = {k: v[layer] for k, v in wp.items()}
        za, zb, zc = _proj_in(x2d, p['wa'], p['wb'], p['wc'], tm)
        za3, zb3, zc3 = za.reshape(b, l, A_W), zb.reshape(b, l, B_W), zc.reshape(b, l, C_W)

        q, kc, v, ckv_new, kr_new = _mla_prep(za3, tabs['mla_q'], tabs['mla_kr'], p['qg'], p['kg'],
                                              p['wuq'], p['wk'], p['wv'], tl)
        mla_past = None
        if past is not None:
            mla_past = _mla_past(past['ckv'][layer], _pad_cols(past['krope'][layer], LANE), p['wk'], p['wv'], 512)
        oa = _mla_attn(q, kc, v, mla_past, tq_mla, causal, first_key)

        qd, kd, vd, qi, ki, wi, kst, vst, kist = _dsa_prep(zb3, tabs['dsa_qk'], tabs['idx_q'], tabs['idx_k'], tl, md)
        dsa_past = None
        if past is not None:
            to_heads = lambda c: _pad_cols(jnp.swapaxes(c.astype(md), 1, 2), HEAD_W)
            ones_col = (jnp.arange(HEAD_W) == DH_B).astype(md)
            dsa_past = (to_heads(past['dsa_k'][layer]), to_heads(past['dsa_v'][layer]) + ones_col,
                        past['dsa_kidx'][layer].astype(md))
        ob = _dsa_attn(qd, qi, wi, kd, vd, ki, dsa_past, tq_dsa, causal, first_key, k_sel)

        if past is not None:
            shift, s0 = past['shift'][layer][:, None, :], past['rwkv'][layer]
        else:
            shift, s0 = jnp.zeros((b, 1, C_W), F32), jnp.zeros((b, H_C, N_C, N_C), F32)
        pre = _rwkv_prep(zc3, shift, p, tl, lo, hi)
        oc, s_last = _rwkv_scan(pre, s0, p, c_rwkv, md)

        x1 = _outproj(oa.reshape(t, -1), ob.reshape(t, -1), oc.reshape(t, -1), x2d, p, tm, l, lo, hi, masked)
        dg = _router(x1, p, tm_moe)
        x2d = _moe(x1, dg, p, tm_moe, l, lo, hi, masked)

        n = hi - lo
        new['ckv'].append(ckv_new[:, lo:hi])
        new['krope'].append(kr_new[:, lo:hi])
        new['dsa_k'].append(kst[:, lo:hi].reshape(b, n, H_B, DH_B))
        new['dsa_v'].append(vst[:, lo:hi].reshape(b, n, H_B, DH_B))
        new['dsa_kidx'].append(kist[:, lo:hi])
        new['rwkv'].append(s_last)
        new['shift'].append(zc3[:, hi - 1, :].astype(F32))
    return x2d.reshape(b, l, D_MODEL), {k: jnp.stack(v) for k, v in new.items()}


def kernel(x_prompt, x_sample, cache_mla_ckv, cache_mla_krope, cache_dsa_k, cache_dsa_v, cache_dsa_kidx, state_rwkv, state_rwkv_shift, meta_tokens, w_in, mla_q_norm, mla_kv_norm, mla_w_uq, mla_w_ukv, rwkv_mu, rwkv_w0, rwkv_w_up, rwkv_a0, rwkv_a_up, rwkv_g_up, rwkv_k_k, rwkv_k_a, rwkv_r_k, rwkv_ln_g, rwkv_ln_b, w_out, ln1_g, ln1_b, ln2_g, ln2_b, router_group, router_group_b, router_expert, router_expert_b, exp_w_gate, exp_w_up, exp_w_down):
    weights = {
        'w_in': w_in, 'mla_q_norm': mla_q_norm, 'mla_kv_norm': mla_kv_norm, 'mla_w_uq': mla_w_uq,
        'mla_w_ukv': mla_w_ukv, 'rwkv_mu': rwkv_mu, 'rwkv_w0': rwkv_w0, 'rwkv_w_up': rwkv_w_up,
        'rwkv_a0': rwkv_a0, 'rwkv_a_up': rwkv_a_up, 'rwkv_g_up': rwkv_g_up, 'rwkv_k_k': rwkv_k_k,
        'rwkv_k_a': rwkv_k_a, 'rwkv_r_k': rwkv_r_k, 'rwkv_ln_g': rwkv_ln_g, 'rwkv_ln_b': rwkv_ln_b,
        'w_out': w_out, 'ln1_g': ln1_g, 'ln1_b': ln1_b, 'ln2_g': ln2_g, 'ln2_b': ln2_b,
        'router_group': router_group, 'router_group_b': router_group_b, 'router_expert': router_expert,
        'router_expert_b': router_expert_b, 'exp_w_gate': exp_w_gate, 'exp_w_up': exp_w_up,
        'exp_w_down': exp_w_down,
    }
    wp = _prepare_weights(weights, BF16)
    wp_sample = _prepare_weights(weights, F32)

    bp, seq, _ = x_prompt.shape
    lo, hi = FRONT_PAD, FRONT_PAD + N_META + seq
    lp = -(-hi // KEY_TILE) * KEY_TILE
    meta = jnp.broadcast_to(meta_tokens.astype(F32)[None], (bp, N_META, D_MODEL))
    xp = jnp.concatenate([jnp.zeros((bp, lo, D_MODEL), F32), meta, x_prompt,
                          jnp.zeros((bp, lp - hi, D_MODEL), F32)], axis=1)
    pos_p = jnp.arange(lp, dtype=jnp.int32) - lo
    yp, new_p = _trunk(xp, pos_p, None, wp, lo=lo, hi=hi, causal=True, k_sel=min(TOPK_MAX, seq // 4))
    y_prompt = yp[:, lo + N_META:hi]

    n_past, n_new = cache_mla_ckv.shape[2], x_sample.shape[1]
    assert n_past % (TILE_UNROLL * KEY_TILE) == 0 and n_new <= CHUNK and n_new % 8 == 0, (n_past, n_new)
    pos_s = n_past + jnp.arange(n_new, dtype=jnp.int32)
    past_s = {'ckv': cache_mla_ckv, 'krope': cache_mla_krope, 'dsa_k': cache_dsa_k, 'dsa_v': cache_dsa_v,
              'dsa_kidx': cache_dsa_kidx, 'rwkv': state_rwkv, 'shift': state_rwkv_shift}
    y_sample, new_s = _trunk(x_sample, pos_s, past_s, wp_sample, lo=0, hi=n_new, causal=False,
                             k_sel=min(TOPK_MAX, (n_past + n_new) // 4))

    keys = ('ckv', 'krope', 'dsa_k', 'dsa_v', 'dsa_kidx', 'rwkv', 'shift')
    return (y_prompt, y_sample) + tuple(new_p[k] for k in keys) + tuple(new_s[k] for k in keys)
```

```python
import functools

import jax
import jax.numpy as jnp
from jax import lax
from jax.experimental import pallas as pl
from jax.experimental.pallas import tpu as pltpu

F32 = jnp.float32
BF16 = jnp.bfloat16
I32 = jnp.int32

D_MODEL = 1024
DEPTH = 4
CHUNK = 64
N_META = 16
ROPE_THETA = 10000.0
NEG_INF = -1e30
LN_EPS = 1e-5
RMS_EPS = 1e-6
GN_EPS = 64e-5
ALPHA = (2 * DEPTH) ** 0.25

H_A, Q_LORA, KV_LORA, NOPE_A, ROPE_A, V_A = 4, 256, 128, 64, 32, 64
H_B, DH_B, H_I, D_I, TOPK_MAX = 4, 64, 8, 32, 256
H_C, N_C, W_C, W_LORA, A_LORA, G_LORA = 8, 64, 512, 32, 32, 64
N_GROUPS, EXP_PER_GROUP, N_EXPERTS, D_EXPERT = 4, 4, 16, 256

A_COLS = Q_LORA + KV_LORA + ROPE_A
B_COLS = 3 * H_B * DH_B + H_I * D_I + D_I + H_I
C_COLS = 3 * W_C + W_LORA + A_LORA + G_LORA

MLA_SCALE = (NOPE_A + ROPE_A) ** -0.5
DSA_SCALE = DH_B ** -0.5
IDX_SCALE = (H_I ** -0.5) * (D_I ** -0.5)

LANE = 128
HEAD_W = 128
A_W = Q_LORA + KV_LORA + LANE
B_Q, B_K, B_V = 0, H_B * HEAD_W, 2 * H_B * HEAD_W
B_QI = 3 * H_B * HEAD_W
B_KI = B_QI + H_I * D_I
B_W = B_KI + LANE
C_W = C_COLS
FRONT_PAD = CHUNK - N_META
KEY_TILE = 256
TILE_UNROLL = 2
INT_MIN = -2147483648
INT_MAX = 2147483647
VMEM_LIMIT = 56 * 1024 * 1024


def _cparams(*sem):
    return pltpu.CompilerParams(dimension_semantics=sem, vmem_limit_bytes=VMEM_LIMIT)


def _full(shape):
    nd = len(shape)
    return pl.BlockSpec(shape, lambda *_: (0,) * nd)


def _round_up(n, m):
    return -(-n // m) * m


def _tile_lanes(t, reps):
    return t if reps == 1 else jnp.concatenate([t] * reps, axis=1)


def _rope(x, tab_ref, half):
    n = x.shape[1]
    reps = n // LANE
    cos = _tile_lanes(tab_ref[0], reps)
    sin_lo = _tile_lanes(tab_ref[1], reps)
    sin_hi = _tile_lanes(tab_ref[2], reps)
    return x * cos + pltpu.roll(x, n - half, 1) * sin_lo + pltpu.roll(x, half, 1) * sin_hi


def _mm(a, b):
    if b.dtype == F32:
        return jnp.dot(a.astype(F32), b, preferred_element_type=F32, precision=lax.Precision.HIGHEST)
    return jnp.dot(a.astype(BF16), b, preferred_element_type=F32)


def _mm_t(a, b):
    prec = lax.Precision.HIGHEST if b.dtype == F32 else None
    return lax.dot_general(a.astype(b.dtype), b, (((1,), (1,)), ((), ())), preferred_element_type=F32,
                           precision=prec)


def _pair_heads(parts):
    lo = parts[0] + pltpu.roll(parts[1], 64, 1)
    hi = parts[2] + pltpu.roll(parts[3], 64, 1)
    return jnp.concatenate([lo, hi], axis=1)


def _proj_in_kernel(x_ref, wa_ref, wb_ref, wc_ref, za_ref, zb_ref, zc_ref):
    xb = x_ref[...].astype(wa_ref.dtype)
    za_ref[...] = _mm(xb, wa_ref[...]).astype(za_ref.dtype)
    zb_ref[...] = _mm(xb, wb_ref[...]).astype(zb_ref.dtype)
    zc_ref[...] = _mm(xb, wc_ref[...]).astype(zc_ref.dtype)


def _proj_in(x2d, wa, wb, wc, tm):
    t = x2d.shape[0]
    row = lambda w: pl.BlockSpec((tm, w), lambda i: (i, 0))
    return pl.pallas_call(
        _proj_in_kernel, grid=(t // tm,),
        in_specs=[row(D_MODEL), _full(wa.shape), _full(wb.shape), _full(wc.shape)],
        out_specs=[row(A_W), row(B_W), row(C_W)],
        out_shape=[jax.ShapeDtypeStruct((t, w), wa.dtype) for w in (A_W, B_W, C_W)],
        compiler_params=_cparams("parallel"), name="proj_in")(x2d, wa, wb, wc)


def _rms(x, g):
    return x * lax.rsqrt(jnp.mean(x * x, axis=-1, keepdims=True) + RMS_EPS) * g


def _mla_prep_kernel(za_ref, tq_ref, tk_ref, qg_ref, kg_ref, wuq_ref, wk_ref, wv_ref,
                     q_ref, kc_ref, v_ref, ckv_ref, kr_ref):
    za = za_ref[...].astype(F32)
    qn = _rms(za[:, :Q_LORA], qg_ref[...])
    q = _rope(_mm(qn, wuq_ref[...]), tq_ref, ROPE_A // 2) * MLA_SCALE
    ckvn = _rms(za[:, Q_LORA:Q_LORA + KV_LORA], kg_ref[...])
    ckv_ref[...] = ckvn
    kr = _rope(za[:, Q_LORA + KV_LORA:], tk_ref, ROPE_A // 2)
    kr_ref[...] = kr[:, :ROPE_A]
    kc = _mm(ckvn, wk_ref[...]) + _tile_lanes(pltpu.roll(kr, NOPE_A, 1), H_A)
    v = _mm(ckvn, wv_ref[...]) + _ones_lane(H_A * HEAD_W)
    for h in range(H_A):
        sl = slice(h * HEAD_W, (h + 1) * HEAD_W)
        q_ref[h] = q[:, sl].astype(q_ref.dtype)
        kc_ref[h] = kc[:, sl].astype(kc_ref.dtype)
        v_ref[h] = v[:, sl].astype(v_ref.dtype)


def _mla_prep(za3, tq, tk, qg, kg, wuq, wk, wv, tl):
    b, l, _ = za3.shape
    heads = pl.BlockSpec((None, H_A, tl, HEAD_W), lambda bi, i: (bi, 0, i, 0))
    tab = pl.BlockSpec((3, tl, LANE), lambda bi, i: (0, i, 0))
    hshape = jax.ShapeDtypeStruct((b, H_A, l, HEAD_W), wuq.dtype)
    return pl.pallas_call(
        _mla_prep_kernel, grid=(b, l // tl),
        in_specs=[pl.BlockSpec((None, tl, A_W), lambda bi, i: (bi, i, 0)), tab, tab,
                  _full(qg.shape), _full(kg.shape), _full(wuq.shape), _full(wk.shape), _full(wv.shape)],
        out_specs=[heads, heads, heads,
                   pl.BlockSpec((None, tl, KV_LORA), lambda bi, i: (bi, i, 0)),
                   pl.BlockSpec((None, tl, ROPE_A), lambda bi, i: (bi, i, 0))],
        out_shape=[hshape, hshape, hshape,
                   jax.ShapeDtypeStruct((b, l, KV_LORA), F32), jax.ShapeDtypeStruct((b, l, ROPE_A), F32)],
        compiler_params=_cparams("parallel", "parallel"), name="mla_prep")(za3, tq, tk, qg, kg, wuq, wk, wv)


def _mla_past_kernel(ckv_ref, kr_ref, wk_ref, wv_ref, kc_ref, v_ref):
    cb = ckv_ref[...]
    kc = _mm(cb, wk_ref[...]) + _tile_lanes(pltpu.roll(kr_ref[...], NOPE_A, 1), H_A)
    v = _mm(cb, wv_ref[...]) + _ones_lane(H_A * HEAD_W)
    for h in range(H_A):
        sl = slice(h * HEAD_W, (h + 1) * HEAD_W)
        kc_ref[h] = kc[:, sl].astype(kc_ref.dtype)
        v_ref[h] = v[:, sl].astype(v_ref.dtype)


def _mla_past(ckv, kr128, wk, wv, tl):
    b, p, _ = ckv.shape
    heads = pl.BlockSpec((None, H_A, tl, HEAD_W), lambda bi, i: (bi, 0, i, 0))
    hshape = jax.ShapeDtypeStruct((b, H_A, p, HEAD_W), wk.dtype)
    return pl.pallas_call(
        _mla_past_kernel, grid=(b, p // tl),
        in_specs=[pl.BlockSpec((None, tl, KV_LORA), lambda bi, i: (bi, i, 0)),
                  pl.BlockSpec((None, tl, LANE), lambda bi, i: (bi, i, 0)), _full(wk.shape), _full(wv.shape)],
        out_specs=[heads, heads], out_shape=[hshape, hshape],
        compiler_params=_cparams("parallel", "parallel"), name="mla_past")(ckv, kr128, wk, wv)


def _ones_lane(width):
    lane = lax.broadcasted_iota(I32, (1, width), 1)
    return jnp.where((lane & (HEAD_W - 1)) == V_A, 1.0, 0.0)


def _fold_lanes_max(s):
    out = s[:, :LANE]
    for c0 in range(LANE, s.shape[1], LANE):
        out = jnp.maximum(out, s[:, c0:c0 + LANE])
    return out


def _tile_groups(nt, masked):
    if isinstance(nt, int):
        unroll = TILE_UNROLL if nt % TILE_UNROLL == 0 else 1
        return nt // unroll, unroll
    assert masked, "a rounded-up tile count needs a mask"
    return (nt + TILE_UNROLL - 1) // TILE_UNROLL, TILE_UNROLL


def _group_offsets(g, u, unroll, tk, last_tile):
    j = g * unroll + u
    off = pl.multiple_of(j * tk, tk)
    if unroll == 1:
        return off, off
    return off, pl.multiple_of(jnp.minimum(j, last_tile) * tk, tk)


def _attend(q_ref, sources, s_scr, tq, n_heads, mask_fn):
    heads = range(n_heads)
    qs = [q_ref[h] for h in heads]
    mrun = tuple(jnp.full((tq, LANE), NEG_INF, F32) for _ in heads)
    for (k_ref, _, base, tk, nt, masked) in sources:
        groups, unroll = _tile_groups(nt, masked)
        last = k_ref.shape[1] // tk - 1

        def score_body(g, ms, k_ref=k_ref, base=base, tk=tk, masked=masked, unroll=unroll, last=last):
            for u in range(unroll):
                off, kv_off = _group_offsets(g, u, unroll, tk, last)
                col = pl.multiple_of(base + off, tk)
                ss = [_mm_t(qs[h], k_ref[h, pl.ds(kv_off, tk), :]) for h in heads]
                if masked:
                    mask = mask_fn(off, col, tk)
                    ss = [jnp.where(mask, s, NEG_INF) for s in ss]
                for h in heads:
                    s_scr[h, :, pl.ds(col, tk)] = ss[h]
                if tk % LANE == 0:
                    ms = tuple(jnp.maximum(ms[h], _fold_lanes_max(ss[h])) for h in heads)
                else:
                    ms = tuple(jnp.maximum(ms[h], jnp.max(ss[h], axis=1, keepdims=True)) for h in heads)
            return ms
        mrun = lax.fori_loop(0, groups, score_body, mrun)
    m = [jnp.max(mr, axis=1, keepdims=True) for mr in mrun]
    accs = tuple(jnp.zeros((tq, HEAD_W), F32) for _ in heads)
    for (_, v_ref, base, tk, nt, masked) in sources:
        groups, unroll = _tile_groups(nt, masked)
        last = v_ref.shape[1] // tk - 1

        def pv_body(g, acc, v_ref=v_ref, base=base, tk=tk, unroll=unroll, last=last):
            for u in range(unroll):
                off, kv_off = _group_offsets(g, u, unroll, tk, last)
                col = pl.multiple_of(base + off, tk)
                ps = [jnp.exp(s_scr[h, :, pl.ds(col, tk)] - m[h]) for h in heads]
                acc = tuple(acc[h] + _mm(ps[h], v_ref[h, pl.ds(kv_off, tk), :]) for h in heads)
            return acc
        accs = lax.fori_loop(0, groups, pv_body, accs)
    lane = lax.broadcasted_iota(I32, (1, HEAD_W), 1)
    return [jnp.where(lane < V_A, a / a[:, V_A:V_A + 1], 0.0) for a in accs]


def _mla_attn_kernel(*refs, tq, tkn, ln, causal, first_key, p_len, tkp):
    if p_len:
        q_ref, kp_ref, vp_ref, kn_ref, vn_ref, o_ref, s_scr = refs
    else:
        q_ref, kn_ref, vn_ref, o_ref, s_scr = refs
    qt = pl.program_id(1)
    if causal:
        row = qt * tq + lax.broadcasted_iota(I32, (tq, 1), 0)
        nvis = ((row >> 6) + 1) << 6
        n_tiles = (qt * tq + tq + tkn - 1) // tkn
    else:
        nvis = None
        n_tiles = ln // tkn
    sources = []
    if p_len:
        sources.append((kp_ref, vp_ref, 0, tkp, p_len // tkp, False))
    sources.append((kn_ref, vn_ref, p_len, tkn, n_tiles, causal))

    def mask_fn(off, col, tk):
        key = off + lax.broadcasted_iota(I32, (1, tk), 1)
        return (key >= first_key) & (key < nvis)

    o_ref[...] = _pair_heads(_attend(q_ref, sources, s_scr, tq, H_A, mask_fn))


def _mla_attn(q, kn, vn, past, tq, causal, first_key):
    b, _, l, _ = q.shape
    ln = kn.shape[2]
    tkn = KEY_TILE if ln % KEY_TILE == 0 else ln
    qspec = pl.BlockSpec((None, H_A, tq, HEAD_W), lambda bi, i: (bi, 0, i, 0))
    whole = lambda n: pl.BlockSpec((None, H_A, n, HEAD_W), lambda bi, i: (bi, 0, 0, 0))
    args, specs, p_len = [q], [qspec], 0
    if past is not None:
        p_len = past[0].shape[2]
        args += list(past)
        specs += [whole(p_len), whole(p_len)]
    args += [kn, vn]
    specs += [whole(ln), whole(ln)]
    kern = functools.partial(_mla_attn_kernel, tq=tq, tkn=tkn, ln=ln, causal=causal, first_key=first_key,
                             p_len=p_len, tkp=KEY_TILE)
    return pl.pallas_call(
        kern, grid=(b, l // tq), in_specs=specs,
        out_specs=pl.BlockSpec((None, tq, H_A * V_A), lambda bi, i: (bi, i, 0)),
        out_shape=jax.ShapeDtypeStruct((b, l, H_A * V_A), F32),
        scratch_shapes=[pltpu.VMEM((H_A, tq, p_len + _round_up(ln, TILE_UNROLL * tkn)), F32)],
        compiler_params=_cparams("parallel", "arbitrary"), name="mla_attn")(*args)


def _dsa_prep_kernel(zb_ref, tqk_ref, tiq_ref, tik_ref,
                     q_ref, k_ref, v_ref, qi_ref, ki_ref, wi_ref, kst_ref, vst_ref, kist_ref):
    zb = zb_ref[...].astype(F32)
    q = _rope(zb[:, B_Q:B_K], tqk_ref, DH_B // 2) * DSA_SCALE
    k = _rope(zb[:, B_K:B_V], tqk_ref, DH_B // 2)
    v = zb[:, B_V:B_QI]
    qi = _rope(zb[:, B_QI:B_KI], tiq_ref, D_I // 2)
    kiw = _rope(zb[:, B_KI:B_W], tik_ref, D_I // 2)
    ks, vs = [], []
    for h in range(H_B):
        sl = slice(h * HEAD_W, (h + 1) * HEAD_W)
        q_ref[h] = q[:, sl].astype(q_ref.dtype)
        k_ref[h] = k[:, sl].astype(k_ref.dtype)
        v_ref[h] = (v[:, sl] + _ones_lane(HEAD_W)).astype(v_ref.dtype)
        ks.append(k[:, sl])
        vs.append(v[:, sl])
    for h in range(H_I):
        qi_ref[h] = qi[:, h * D_I:(h + 1) * D_I].astype(qi_ref.dtype)
    ki_ref[...] = kiw[:, :D_I].astype(ki_ref.dtype)
    wi_ref[...] = kiw * IDX_SCALE
    kst_ref[...] = _pair_heads(ks)
    vst_ref[...] = _pair_heads(vs)
    kist_ref[...] = kiw[:, :D_I]


def _dsa_prep(zb3, tqk, tiq, tik, tl, mm_dtype):
    b, l, _ = zb3.shape
    heads = pl.BlockSpec((None, H_B, tl, HEAD_W), lambda bi, i: (bi, 0, i, 0))
    tab = pl.BlockSpec((3, tl, LANE), lambda bi, i: (0, i, 0))
    rows = lambda w: pl.BlockSpec((None, tl, w), lambda bi, i: (bi, i, 0))
    hshape = jax.ShapeDtypeStruct((b, H_B, l, HEAD_W), mm_dtype)
    return pl.pallas_call(
        _dsa_prep_kernel, grid=(b, l // tl),
        in_specs=[rows(B_W), tab, tab, tab],
        out_specs=[heads, heads, heads,
                   pl.BlockSpec((None, H_I, tl, D_I), lambda bi, i: (bi, 0, i, 0)),
                   rows(D_I), rows(LANE), rows(H_B * DH_B), rows(H_B * DH_B), rows(D_I)],
        out_shape=[hshape, hshape, hshape,
                   jax.ShapeDtypeStruct((b, H_I, l, D_I), mm_dtype),
                   jax.ShapeDtypeStruct((b, l, D_I), mm_dtype),
                   jax.ShapeDtypeStruct((b, l, LANE), F32),
                   jax.ShapeDtypeStruct((b, l, H_B * DH_B), F32),
                   jax.ShapeDtypeStruct((b, l, H_B * DH_B), F32),
                   jax.ShapeDtypeStruct((b, l, D_I), F32)],
        compiler_params=_cparams("parallel", "parallel"), name="dsa_prep")(zb3, tqk, tiq, tik)


def _order_key(x):
    x = jnp.where(x == 0.0, 0.0, x)
    b = pltpu.bitcast(x, I32)
    return jnp.where(b < 0, b ^ INT_MAX, b)


def _dsa_attn_kernel(*refs, tq, tkn, ln, causal, first_key, p_len, tkp, k_sel):
    if p_len:
        (q_ref, qi_ref, wi_ref, kp_ref, vp_ref, kip_ref, kn_ref, vn_ref, kin_ref, o_ref,
         keys_ref, s_scr, wrep_scr) = refs
    else:
        (q_ref, qi_ref, wi_ref, kn_ref, vn_ref, kin_ref, o_ref, keys_ref, s_scr, wrep_scr) = refs
    qt = pl.program_id(1)
    p_tiles = p_len // tkp if p_len else 0
    if causal:
        row = qt * tq + lax.broadcasted_iota(I32, (tq, 1), 0)
        nvis = ((row >> 6) + 1) << 6
        n_tiles = (qt * tq + tq + tkn - 1) // tkn
        n_valid = qt * tq + tq - first_key + p_len
    else:
        nvis = ln
        n_tiles = ln // tkn
        n_valid = ln + p_len

    sources = []
    if p_len:
        sources.append((kp_ref, vp_ref, kip_ref, 0, tkp, p_tiles, False))
    sources.append((kn_ref, vn_ref, kin_ref, p_len, tkn, n_tiles, causal))

    def col_mask(off, tk):
        col = off + lax.broadcasted_iota(I32, (1, tk), 1)
        return (col >= first_key) & (col < nvis)

    wi = wi_ref[...]
    qis = [qi_ref[h] for h in range(H_I)]
    for h in range(H_I):
        wrep_scr[h] = jnp.broadcast_to(wi[:, D_I + h:D_I + h + 1], (tq, LANE))
    for (_, _, ki_ref, base, tk, nt, masked) in sources:
        groups, unroll = _tile_groups(nt, masked)
        last = ki_ref.shape[0] // tk - 1

        def score_body(g, c, ki_ref=ki_ref, base=base, tk=tk, masked=masked, unroll=unroll, last=last):
            wts = [_tile_lanes(wrep_scr[h], tk // LANE) if tk >= LANE else wrep_scr[h][:, :tk] for h in range(H_I)]
            for u in range(unroll):
                off, kv_off = _group_offsets(g, u, unroll, tk, last)
                ki_t = ki_ref[pl.ds(kv_off, tk), :]
                sc = jnp.maximum(_mm_t(qis[0], ki_t), 0.0) * wts[0]
                for h in range(1, H_I):
                    sc = sc + jnp.maximum(_mm_t(qis[h], ki_t), 0.0) * wts[h]
                key = _order_key(sc)
                if masked:
                    key = jnp.where(col_mask(off, tk), key, INT_MIN)
                keys_ref[:, pl.ds(pl.multiple_of(base + off, tk), tk)] = key
            return c
        lax.fori_loop(0, groups, score_body, 0)

    count_sources = []
    for (_, _, _, base, tk, nt, _) in sources:
        if tk == KEY_TILE:
            count_sources.append((base, TILE_UNROLL * tk, _tile_groups(nt, True)[0] if not isinstance(nt, int)
                                  else nt // TILE_UNROLL))
        else:
            count_sources.append((base, tk, nt))

    def count_ge(mid):
        total = jnp.zeros((tq, 1), F32)
        for (base, tk, nt) in count_sources:
            wacc = min(tk, LANE)

            def cnt_body(j, acc, base=base, tk=tk, wacc=wacc):
                off = pl.multiple_of(base + j * tk, tk)
                hit = jnp.where(keys_ref[:, pl.ds(off, tk)] >= mid, 1.0, 0.0)
                for c0 in range(0, tk, wacc):
                    acc = acc + hit[:, c0:c0 + wacc]
                return acc
            acc = lax.fori_loop(0, nt, cnt_body, jnp.zeros((tq, wacc), F32))
            total = total + jnp.sum(acc, axis=1, keepdims=True)
        return total

    def bisect_body(_, c):
        lo, hi, n_lo = c
        mid = (lo >> 1) + (hi >> 1) + (((lo & 1) + (hi & 1) + 1) >> 1)
        n_mid = count_ge(mid)
        ge = n_mid >= float(k_sel)
        return jnp.where(ge, mid, lo), jnp.where(ge, hi, mid - 1), jnp.where(ge, n_mid, n_lo)

    n_iter = jnp.where(n_valid > k_sel, 32, 0)
    thr, _, n_sel = lax.fori_loop(0, n_iter, bisect_body,
                                  (jnp.full((tq, 1), INT_MIN + 1, I32), jnp.full((tq, 1), INT_MAX, I32),
                                   jnp.zeros((tq, 1), F32)))

    has_ties = jnp.max(jnp.where(n_sel > float(k_sel), 1.0, 0.0)) > 0.0

    @pl.when(has_ties)
    def _():
        keep = float(k_sel) - count_ge(thr + 1)
        run = jnp.zeros((tq, 1), F32)
        for (_, _, _, base, tk, nt, _) in sources:
            upper = jnp.where(lax.broadcasted_iota(I32, (tk, tk), 0) <= lax.broadcasted_iota(I32, (tk, tk), 1),
                              1.0, 0.0).astype(BF16)

            def tie_body(j, run, base=base, tk=tk, upper=upper):
                col = pl.multiple_of(base + j * tk, tk)
                kt = keys_ref[:, pl.ds(col, tk)]
                eq = kt == thr
                eqf = jnp.where(eq, 1.0, 0.0)
                rank = run + _mm(eqf, upper)
                keys_ref[:, pl.ds(col, tk)] = jnp.where(eq & (rank > keep), INT_MIN, kt)
                return run + jnp.sum(eqf, axis=1, keepdims=True)
            run = lax.fori_loop(0, nt, tie_body, run)

    def selected(off, col, tk):
        return keys_ref[:, pl.ds(col, tk)] >= thr

    att_sources = [(k_ref, v_ref, base, tk, nt, True) for (k_ref, v_ref, _, base, tk, nt, _) in sources]
    o_ref[...] = _pair_heads(_attend(q_ref, att_sources, s_scr, tq, H_B, selected))


def _dsa_attn(q, qi, wi, kn, vn, kin, past, tq, causal, first_key, k_sel):
    b, _, l, _ = q.shape
    ln = kn.shape[2]
    tkn = KEY_TILE if ln % KEY_TILE == 0 else ln
    qspec = pl.BlockSpec((None, H_B, tq, HEAD_W), lambda bi, i: (bi, 0, i, 0))
    whole = lambda n: pl.BlockSpec((None, H_B, n, HEAD_W), lambda bi, i: (bi, 0, 0, 0))
    whole_ki = lambda n: pl.BlockSpec((None, n, D_I), lambda bi, i: (bi, 0, 0))
    args = [q, qi, wi]
    specs = [qspec, pl.BlockSpec((None, H_I, tq, D_I), lambda bi, i: (bi, 0, i, 0)),
             pl.BlockSpec((None, tq, LANE), lambda bi, i: (bi, i, 0))]
    p_len = 0
    if past is not None:
        p_len = past[0].shape[2]
        args += list(past)
        specs += [whole(p_len), whole(p_len), whole_ki(p_len)]
    args += [kn, vn, kin]
    specs += [whole(ln), whole(ln), whole_ki(ln)]
    kern = functools.partial(_dsa_attn_kernel, tq=tq, tkn=tkn, ln=ln, causal=causal, first_key=first_key,
                             p_len=p_len, tkp=KEY_TILE, k_sel=k_sel)
    return pl.pallas_call(
        kern, grid=(b, l // tq), in_specs=specs,
        out_specs=pl.BlockSpec((None, tq, H_B * DH_B), lambda bi, i: (bi, i, 0)),
        out_shape=jax.ShapeDtypeStruct((b, l, H_B * DH_B), F32),
        scratch_shapes=[pltpu.VMEM((tq, p_len + _round_up(ln, TILE_UNROLL * tkn)), I32),
                        pltpu.VMEM((H_B, tq, p_len + _round_up(ln, TILE_UNROLL * tkn)), F32),
                        pltpu.VMEM((H_I, tq, LANE), F32)],
        compiler_params=_cparams("parallel", "arbitrary"), name="dsa_attn")(*args)


def _softplus(x):
    return jnp.maximum(x, 0.0) + jnp.log(1.0 + jnp.exp(-jnp.abs(x)))


def _rwkv_prep_kernel(zc_ref, zp_ref, sh_ref, mu_ref, w0_ref, a0_ref, kkw_ref, ka_ref,
                      wup_ref, aup_ref, gup_ref, bd_ref,
                      r_ref, k_ref, v_ref, ld_ref, kk_ref, b_ref, g_ref, *, tl, lo, hi):
    i = pl.program_id(1)
    z = zc_ref[...].astype(F32)
    rowi = lax.broadcasted_iota(I32, (tl, 1), 0)
    last = zp_ref.shape[0] - 1
    prev_last = jnp.where(i == 0, sh_ref[...], zp_ref[last:last + 1, :].astype(F32))
    zprev = jnp.where(rowi == 0, prev_last, pltpu.roll(z, 1, 0))
    zs = z + (zprev - z) * mu_ref[...]
    r, k, v, lora = zs[:, :W_C], zs[:, W_C:2 * W_C], zs[:, 2 * W_C:3 * W_C], zs[:, 3 * W_C:]
    lane = lax.broadcasted_iota(I32, (1, LANE), 1)
    u = jnp.where(lane < W_LORA, jnp.tanh(lora),
                  jnp.where(lane < W_LORA + A_LORA, lora, jax.nn.sigmoid(lora)))
    w = -_softplus(-(w0_ref[...] + _mm(u, wup_ref[...]))) - 0.5
    ld = -jnp.exp(w)
    a = jax.nn.sigmoid(a0_ref[...] + _mm(u, aup_ref[...]))
    g = _mm(u, gup_ref[...])
    kmod = k * (1.0 + (a - 1.0) * ka_ref[...])
    kk = k * kkw_ref[...]
    sq = kk * kk
    bd = bd_ref[...]
    if bd.dtype == F32:
        ss = _mm(sq, bd)
    else:
        sq_hi = sq.astype(BF16)
        ss = _mm(sq_hi, bd) + _mm(sq - sq_hi.astype(F32), bd)
    kkn = kk * jnp.minimum(lax.rsqrt(ss), 1e12)
    grow = i * tl + rowi
    valid = ((grow >= lo) & (grow < hi)).astype(F32)
    r_ref[...] = r.astype(r_ref.dtype)
    k_ref[...] = (kmod * valid).astype(k_ref.dtype)
    v_ref[...] = (v * valid).astype(v_ref.dtype)
    ld_ref[...] = ld * valid
    kk_ref[...] = (kkn * valid).astype(kk_ref.dtype)
    b_ref[...] = (kkn * a * valid).astype(b_ref.dtype)
    g_ref[...] = g.astype(g_ref.dtype)


def _rwkv_prep(zc3, shift, p, tl, lo, hi):
    b, l, _ = zc3.shape
    rows = lambda w: pl.BlockSpec((None, tl, w), lambda bi, i: (bi, i, 0))
    vec = _full((1, W_C))
    mat = _full((LANE, W_C))
    kern = functools.partial(_rwkv_prep_kernel, tl=tl, lo=lo, hi=hi)
    zdt = zc3.dtype
    pr = 8 if zdt == F32 else 16
    out = lambda dt: jax.ShapeDtypeStruct((b, l, W_C), dt)
    return pl.pallas_call(
        kern, grid=(b, l // tl),
        in_specs=[rows(C_W),
                  pl.BlockSpec((None, pr, C_W), lambda bi, i: (bi, jnp.maximum(i * (tl // pr) - 1, 0), 0)),
                  pl.BlockSpec((None, 1, C_W), lambda bi, i: (bi, 0, 0)),
                  _full((1, C_W)), vec, vec, vec, vec, mat, mat, mat, _full((W_C, W_C))],
        out_specs=[rows(W_C)] * 7,
        out_shape=[out(zdt), out(zdt), out(zdt), out(F32), out(zdt), out(zdt), out(zdt)],
        compiler_params=_cparams("parallel", "parallel"), name="rwkv_prep")(
            zc3, zc3, shift, p['mu'], p['w0'], p['a0'], p['k_k'], p['k_a'], p['wup'], p['aup'], p['gup'], p['bd'])


def _rwkv_scan_kernel(r_ref, k_ref, v_ref, ld_ref, kk_ref, b_ref, g_ref, s0_ref, lng_ref, lnb_ref, rk_ref,
                      o_ref, sout_ref, s_scr, *, c, n_chunks, md, nb):
    ci = pl.program_id(1)

    @pl.when(ci == 0)
    def _():
        s_scr[...] = s0_ref[...]

    rowi = lax.broadcasted_iota(I32, (c, 1), 0)
    ri = lax.broadcasted_iota(I32, (c, c), 0)
    cj = lax.broadcasted_iota(I32, (c, c), 1)
    strict, incl = ri > cj, ri >= cj
    eye = jnp.where(ri == cj, 1.0, 0.0)
    n_sq = c.bit_length() - 2

    rows = []
    for bi in range(nb):
        ld = ld_ref[bi]
        cum = ld
        step = 1
        while step < c:
            cum = cum + jnp.where(rowi >= step, pltpu.roll(cum, step, 0), 0.0)
            step *= 2
        cl = cum[c - 1:c, :]
        e_in, e_ex, e_neg, e_end = jnp.exp(cum), jnp.exp(cum - ld), jnp.exp(-cum), jnp.exp(cl - cum)
        r, k, v, kk, bb = (x[bi].astype(F32) for x in (r_ref, k_ref, v_ref, kk_ref, b_ref))
        rows.append(dict(d_end=jnp.exp(cl), v=v, g=g_ref[bi].astype(F32),
                         rt=r * e_in, kap=kk * e_ex, kt=k * e_neg, bt=bb * e_neg,
                         kte=k * e_end, bte=bb * e_end, bonus_rk=r * k * rk_ref[...]))

    units = [(bi, h) for bi in range(nb) for h in range(H_C)]
    idx = range(len(units))
    sls = [slice(h * N_C, (h + 1) * N_C) for _, h in units]
    s_old = [s_scr[bi, h] for bi, h in units]
    s_mm = [s.astype(md) for s in s_old]
    left = [jnp.concatenate([rows[bi]['kap'][:, sls[u]], rows[bi]['rt'][:, sls[u]]], axis=0).astype(md)
            for u, (bi, _) in enumerate(units)]
    right = [jnp.concatenate([rows[bi]['bt'][:, sls[u]], rows[bi]['kt'][:, sls[u]]], axis=0).astype(md)
             for u, (bi, _) in enumerate(units)]
    gram = [_mm_t(left[u], right[u]) for u in idx]
    a_kb = [jnp.where(strict, g[:c, :c], 0.0) for g in gram]
    a_kk = [jnp.where(strict, g[:c, c:], 0.0) for g in gram]
    a_rb = [jnp.where(incl, g[c:, :c], 0.0) for g in gram]
    a_rk = [jnp.where(incl, g[c:, c:], 0.0) for g in gram]
    vh = [rows[bi]['v'][:, sls[u]] for u, (bi, _) in enumerate(units)]
    vb = [x.astype(md) for x in vh]
    x0 = [_mm_t(left[u][:c], s_mm[u]) + _mm(a_kk[u], vb[u]) for u in idx]
    y0 = [_mm_t(left[u][c:], s_mm[u]) + _mm(a_rk[u], vb[u]) for u in idx]
    tinv = [eye - n for n in a_kb]
    npow = a_kb
    for _ in range(n_sq):
        npow_mm = [n.astype(md) for n in npow]
        npow = [_mm(n, n) for n in npow_mm]
        tinv = [tinv[u] + _mm(tinv[u], npow[u].astype(md)) for u in idx]
    w = [_mm(tinv[u], x0[u].astype(md)) for u in idx]
    wb = [x.astype(md) for x in w]
    y = [y0[u] - _mm(a_rb[u], wb[u]) for u in idx]
    for u, (bi, h) in enumerate(units):
        sl = sls[u]
        s_scr[bi, h] = (s_old[u] * rows[bi]['d_end'][:, sl] + _mm(vh[u].T, rows[bi]['kte'][:, sl].astype(md))
                        - _mm(w[u].T, rows[bi]['bte'][:, sl].astype(md)))
    for u, (bi, h) in enumerate(units):
        sl = sls[u]
        ym = jnp.mean(y[u], axis=-1, keepdims=True)
        yc = y[u] - ym
        yn = yc * lax.rsqrt(jnp.mean(yc * yc, axis=-1, keepdims=True) + GN_EPS) * lng_ref[:, sl] + lnb_ref[:, sl]
        bonus = jnp.sum(rows[bi]['bonus_rk'][:, sl], axis=-1, keepdims=True) * vh[u]
        o_ref[bi, :, sl] = (yn + bonus) * rows[bi]['g'][:, sl]

    @pl.when(ci == n_chunks - 1)
    def _():
        sout_ref[...] = s_scr[...]


def _rwkv_scan(pre, s0, p, c, mm_dtype):
    r = pre[0]
    b, l, _ = r.shape
    n_chunks = l // c
    nb = 2 if b % 2 == 0 else 1
    rows = pl.BlockSpec((nb, c, W_C), lambda bi, i: (bi, i, 0))
    state = pl.BlockSpec((nb, H_C, N_C, N_C), lambda bi, i: (bi, 0, 0, 0))
    vec = _full((1, W_C))
    kern = functools.partial(_rwkv_scan_kernel, c=c, n_chunks=n_chunks, md=mm_dtype, nb=nb)
    return pl.pallas_call(
        kern, grid=(b // nb, n_chunks),
        in_specs=[rows] * 7 + [state, vec, vec, vec],
        out_specs=[rows, state],
        out_shape=[jax.ShapeDtypeStruct((b, l, W_C), F32), jax.ShapeDtypeStruct((b, H_C, N_C, N_C), F32)],
        scratch_shapes=[pltpu.VMEM((nb, H_C, N_C, N_C), F32)],
        compiler_params=_cparams("parallel", "arbitrary"), name="rwkv_scan")(
            *pre, s0, p['ln_g'], p['ln_b'], p['r_k'])


def _layernorm(y, g, b):
    mu = jnp.mean(y, axis=-1, keepdims=True)
    yc = y - mu
    return yc * lax.rsqrt(jnp.mean(yc * yc, axis=-1, keepdims=True) + LN_EPS) * g + b


def _row_valid(i, tm, tiles_per_batch, lo, hi):
    rowb = (i % tiles_per_batch) * tm + lax.broadcasted_iota(I32, (tm, 1), 0)
    return (rowb >= lo) & (rowb < hi)


def _outproj_kernel(oa_ref, ob_ref, oc_ref, x_ref, wa_ref, wb_ref, wc_ref, g_ref, b_ref, y_ref,
                    *, tm, tiles_per_batch, lo, hi, masked):
    mix = _mm(oa_ref[...], wa_ref[...]) + _mm(ob_ref[...], wb_ref[...]) + _mm(oc_ref[...], wc_ref[...])
    y = _layernorm(ALPHA * x_ref[...] + mix, g_ref[...], b_ref[...])
    if masked:
        y = jnp.where(_row_valid(pl.program_id(0), tm, tiles_per_batch, lo, hi), y, 0.0)
    y_ref[...] = y


def _outproj(oa, ob, oc, x2d, p, tm, rows_per_batch, lo, hi, masked):
    t = x2d.shape[0]
    row = lambda w: pl.BlockSpec((tm, w), lambda i: (i, 0))
    kern = functools.partial(_outproj_kernel, tm=tm, tiles_per_batch=rows_per_batch // tm, lo=lo, hi=hi,
                             masked=masked)
    return pl.pallas_call(
        kern, grid=(t // tm,),
        in_specs=[row(oa.shape[1]), row(ob.shape[1]), row(oc.shape[1]), row(D_MODEL),
                  _full(p['wo_a'].shape), _full(p['wo_b'].shape), _full(p['wo_c'].shape),
                  _full((1, D_MODEL)), _full((1, D_MODEL))],
        out_specs=row(D_MODEL), out_shape=jax.ShapeDtypeStruct((t, D_MODEL), F32),
        compiler_params=_cparams("parallel"), name="outproj_ln")(
            oa, ob, oc, x2d, p['wo_a'], p['wo_b'], p['wo_c'], p['ln1_g'], p['ln1_b'])


def _router_kernel(x_ref, wr_ref, br_ref, dg_ref, *, full_precision):
    x = x_ref[...]
    w = wr_ref[...]
    if full_precision:
        logits = _mm(x, w) + br_ref[...]
    else:
        xh, wh = x.astype(BF16), w.astype(BF16)
        xl = (x - xh.astype(F32)).astype(BF16)
        wl = (w - wh.astype(F32)).astype(BF16)
        logits = _mm(xh, wh) + _mm(xl, wh) + _mm(xh, wl) + br_ref[...]
    lane = lax.broadcasted_iota(I32, (1, LANE), 1).astype(F32)
    big = float(LANE)
    gm = (lane >= N_EXPERTS) & (lane < N_EXPERTS + N_GROUPS)
    gl = jnp.where(gm, logits, -jnp.inf)
    ge = jnp.exp(gl - jnp.max(gl, axis=1, keepdims=True))
    gp = ge / jnp.sum(ge, axis=1, keepdims=True)
    gval = jnp.max(gp, axis=1, keepdims=True)
    gidx = jnp.min(jnp.where(gm & (gp == gval), lane, big), axis=1, keepdims=True) - N_EXPERTS
    elo = gidx * EXP_PER_GROUP
    em = (lane >= elo) & (lane < elo + EXP_PER_GROUP)
    el = jnp.where(em, logits, -jnp.inf)
    ee = jnp.exp(el - jnp.max(el, axis=1, keepdims=True))
    ep = jnp.where(em, ee / jnp.sum(ee, axis=1, keepdims=True), -1.0)
    p1 = jnp.max(ep, axis=1, keepdims=True)
    i1 = jnp.min(jnp.where(ep == p1, lane, big), axis=1, keepdims=True)
    ep2 = jnp.where(lane == i1, -1.0, ep)
    p2 = jnp.max(ep2, axis=1, keepdims=True)
    i2 = jnp.min(jnp.where((ep2 == p2) & (lane != i1), lane, big), axis=1, keepdims=True)
    den = p1 + p2
    dg_ref[...] = jnp.where(lane == i1, gval * p1 / den, 0.0) + jnp.where(lane == i2, gval * p2 / den, 0.0)


def _router(x2d, p, tm):
    t = x2d.shape[0]
    row = lambda w: pl.BlockSpec((tm, w), lambda i: (i, 0))
    kern = functools.partial(_router_kernel, full_precision=(p['wa'].dtype == F32))
    return pl.pallas_call(
        kern, grid=(t // tm,),
        in_specs=[row(D_MODEL), _full((D_MODEL, LANE)), _full((1, LANE))],
        out_specs=row(LANE), out_shape=jax.ShapeDtypeStruct((t, LANE), F32),
        compiler_params=_cparams("parallel"), name="router")(x2d, p['wr'], p['br'])


def _moe_kernel(x_ref, dg_ref, wgu_ref, wd_ref, g_ref, b_ref, y_ref, acc_ref, xb_ref,
                *, tm, tiles_per_batch, lo, hi, masked, fused_down):
    e = pl.program_id(1)

    @pl.when(e == 0)
    def _():
        xb_ref[...] = x_ref[...].astype(xb_ref.dtype)
        if not fused_down:
            acc_ref[...] = jnp.zeros_like(acc_ref)

    hcat = _mm(xb_ref[...], wgu_ref[...])
    hg, hu = hcat[:, :D_EXPERT], hcat[:, D_EXPERT:]
    lane = lax.broadcasted_iota(I32, (1, LANE), 1)
    gate = jnp.sum(jnp.where(lane == e, dg_ref[...], 0.0), axis=1, keepdims=True)
    hidden = hg * jax.nn.sigmoid(hg) * hu * gate
    if fused_down:
        acc_ref[:, pl.ds(pl.multiple_of(e * D_EXPERT, D_EXPERT), D_EXPERT)] = hidden.astype(acc_ref.dtype)
    else:
        acc_ref[...] += _mm(hidden, wd_ref[...])

    @pl.when(e == N_EXPERTS - 1)
    def _():
        ffn = _mm(acc_ref[...], wd_ref[...]) if fused_down else acc_ref[...]
        y = _layernorm(ALPHA * x_ref[...] + ffn, g_ref[...], b_ref[...])
        if masked:
            y = jnp.where(_row_valid(pl.program_id(0), tm, tiles_per_batch, lo, hi), y, 0.0)
        y_ref[...] = y


def _moe(x2d, dg, p, tm, rows_per_batch, lo, hi, masked):
    t = x2d.shape[0]
    row = lambda w: pl.BlockSpec((tm, w), lambda i, e: (i, 0))
    md = p['wgu'].dtype
    fused_down = md == BF16
    kern = functools.partial(_moe_kernel, tm=tm, tiles_per_batch=rows_per_batch // tm, lo=lo, hi=hi,
                             masked=masked, fused_down=fused_down)
    if fused_down:
        wd = p['wd'].reshape(N_EXPERTS * D_EXPERT, D_MODEL)
        wd_spec = pl.BlockSpec(wd.shape, lambda i, e: (0, 0), pipeline_mode=pl.Buffered(1))
        acc = pltpu.VMEM((tm, N_EXPERTS * D_EXPERT), md)
    else:
        wd = p['wd']
        wd_spec = pl.BlockSpec((None, D_EXPERT, D_MODEL), lambda i, e: (e, 0, 0))
        acc = pltpu.VMEM((tm, D_MODEL), F32)
    return pl.pallas_call(
        kern, grid=(t // tm, N_EXPERTS),
        in_specs=[row(D_MODEL), row(LANE),
                  pl.BlockSpec((None, D_MODEL, 2 * D_EXPERT), lambda i, e: (e, 0, 0)), wd_spec,
                  pl.BlockSpec((1, D_MODEL), lambda i, e: (0, 0)), pl.BlockSpec((1, D_MODEL), lambda i, e: (0, 0))],
        out_specs=row(D_MODEL), out_shape=jax.ShapeDtypeStruct((t, D_MODEL), F32),
        scratch_shapes=[acc, pltpu.VMEM((tm, D_MODEL), md)],
        compiler_params=_cparams("parallel", "arbitrary"), name="moe_ln")(
            x2d, dg, p['wgu'], wd, p['ln2_g'], p['ln2_b'])


def _rope_table(pos, pattern):
    posf = pos.astype(F32)[:, None]
    n = pos.shape[0]
    cos, s_lo, s_hi = [], [], []
    for kind, w in pattern:
        if kind == 'rope':
            half = w // 2
            inv = ROPE_THETA ** (-jnp.arange(half, dtype=F32) / half)
            ang = posf * inv[None, :]
            c, s, z = jnp.cos(ang), jnp.sin(ang), jnp.zeros((n, half), F32)
            cos += [c, c]
            s_lo += [-s, z]
            s_hi += [z, s]
        else:
            fill = jnp.full((n, w), 1.0 if kind == 'one' else 0.0, F32)
            z = jnp.zeros((n, w), F32)
            cos.append(fill)
            s_lo.append(z)
            s_hi.append(z)
    return jnp.stack([jnp.concatenate(t, axis=1) for t in (cos, s_lo, s_hi)])


def _tables(pos):
    return {
        'mla_q': _rope_table(pos, [('one', NOPE_A), ('rope', ROPE_A), ('zero', HEAD_W - NOPE_A - ROPE_A)]),
        'mla_kr': _rope_table(pos, [('rope', ROPE_A), ('zero', LANE - ROPE_A)]),
        'dsa_qk': _rope_table(pos, [('rope', DH_B), ('zero', HEAD_W - DH_B)]),
        'idx_q': _rope_table(pos, [('rope', D_I)] * (LANE // D_I)),
        'idx_k': _rope_table(pos, [('rope', D_I), ('one', H_I), ('zero', LANE - D_I - H_I)]),
    }


def _pad_cols(w, width):
    return jnp.pad(w, [(0, 0)] * (w.ndim - 1) + [(0, width - w.shape[-1])])


def _head_pad(w, n_heads, width):
    d = w.shape[-1] // n_heads
    w = w.reshape(w.shape[:-1] + (n_heads, d))
    return _pad_cols(w, width).reshape(w.shape[:-2] + (n_heads * width,))


def _prepare_weights(w, md):
    w_in = w['w_in']
    a, bseg, cseg = w_in[..., :A_COLS], w_in[..., A_COLS:A_COLS + B_COLS], w_in[..., A_COLS + B_COLS:]
    hd = H_B * DH_B
    wb = jnp.concatenate([
        _head_pad(bseg[..., 0:hd], H_B, HEAD_W), _head_pad(bseg[..., hd:2 * hd], H_B, HEAD_W),
        _head_pad(bseg[..., 2 * hd:3 * hd], H_B, HEAD_W), bseg[..., 3 * hd:3 * hd + H_I * D_I],
        _pad_cols(bseg[..., 3 * hd + H_I * D_I:], LANE)], axis=-1)
    w_uq = w['mla_w_uq'].reshape(DEPTH, Q_LORA, H_A, NOPE_A + ROPE_A)
    w_ukv = w['mla_w_ukv'].reshape(DEPTH, KV_LORA, H_A, NOPE_A + V_A)
    zeros_lora = lambda n: jnp.zeros((DEPTH, n, W_C), F32)
    head_of = jnp.arange(W_C) // N_C
    wr = jnp.concatenate([w['router_expert'], w['router_group']], axis=-1)
    wr = _pad_cols(wr, LANE)
    row = lambda v: v.reshape(DEPTH, 1, -1)
    return {
        'wa': _pad_cols(a, A_W).astype(md), 'wb': wb.astype(md), 'wc': cseg.astype(md),
        'qg': row(w['mla_q_norm']), 'kg': row(w['mla_kv_norm']),
        'wuq': _pad_cols(w_uq, HEAD_W).reshape(DEPTH, Q_LORA, H_A * HEAD_W).astype(md),
        'wk': _pad_cols(w_ukv[..., :NOPE_A], HEAD_W).reshape(DEPTH, KV_LORA, H_A * HEAD_W).astype(md),
        'wv': _pad_cols(w_ukv[..., NOPE_A:], HEAD_W).reshape(DEPTH, KV_LORA, H_A * HEAD_W).astype(md),
        'mu': row(w['rwkv_mu']), 'w0': row(w['rwkv_w0']), 'a0': row(w['rwkv_a0']),
        'k_k': row(w['rwkv_k_k']), 'k_a': row(w['rwkv_k_a']), 'r_k': row(w['rwkv_r_k']),
        'ln_g': row(w['rwkv_ln_g']), 'ln_b': row(w['rwkv_ln_b']),
        'wup': jnp.concatenate([w['rwkv_w_up'], zeros_lora(LANE - W_LORA)], axis=1).astype(md),
        'aup': jnp.concatenate([zeros_lora(W_LORA), w['rwkv_a_up'], zeros_lora(G_LORA)], axis=1).astype(md),
        'gup': jnp.concatenate([zeros_lora(W_LORA + A_LORA), w['rwkv_g_up']], axis=1).astype(md),
        'bd': jnp.broadcast_to((head_of[:, None] == head_of[None, :]).astype(md), (DEPTH, W_C, W_C)),
        'wo_a': w['w_out'][:, :H_A * V_A].astype(md),
        'wo_b': w['w_out'][:, H_A * V_A:H_A * V_A + hd].astype(md),
        'wo_c': w['w_out'][:, H_A * V_A + hd:].astype(md),
        'ln1_g': row(w['ln1_g']), 'ln1_b': row(w['ln1_b']), 'ln2_g': row(w['ln2_g']), 'ln2_b': row(w['ln2_b']),
        'wr': wr,
        'br': _pad_cols(jnp.concatenate([w['router_expert_b'], w['router_group_b']], axis=-1), LANE).reshape(DEPTH, 1, LANE),
        'wgu': jnp.concatenate([w['exp_w_gate'], w['exp_w_up']], axis=-1).astype(md),
        'wd': w['exp_w_down'].astype(md),
    }


def _largest_tile(n, limit):
    best = 8
    for t in range(8, min(n, limit) + 1, 8):
        if n % t == 0:
            best = t
    return best


def _trunk(x, pos, past, wp, *, lo, hi, causal, k_sel):
    b, l, _ = x.shape
    t = b * l
    tabs = _tables(pos)
    tl = min(256, l)
    tq_mla = min(4 * CHUNK, l)
    tq_dsa = min(2 * CHUNK, l)
    c_rwkv = min(CHUNK, l)
    masked = not (lo == 0 and hi == l)
    md = wp['wa'].dtype
    tm_cap = 512 if md == BF16 else 128
    tm = _largest_tile(l, tm_cap) if masked else _largest_tile(t, tm_cap)
    tm_moe = _largest_tile(l, 1100) if masked else _largest_tile(t, 1100)
    first_key = lo if causal else 0
    new = {k: [] for k in ('ckv', 'krope', 'dsa_k', 'dsa_v', 'dsa_kidx', 'rwkv', 'shift')}
    x2d = x.reshape(t, D_MODEL)
    for layer in range(DEPTH):
        p = {k: v[layer] for k, v in wp.items()}
        za, zb, zc = _proj_in(x2d, p['wa'], p['wb'], p['wc'], tm)
        za3, zb3, zc3 = za.reshape(b, l, A_W), zb.reshape(b, l, B_W), zc.reshape(b, l, C_W)

        q, kc, v, ckv_new, kr_new = _mla_prep(za3, tabs['mla_q'], tabs['mla_kr'], p['qg'], p['kg'],
                                              p['wuq'], p['wk'], p['wv'], tl)
        mla_past = None
        if past is not None:
            mla_past = _mla_past(past['ckv'][layer], _pad_cols(past['krope'][layer], LANE), p['wk'], p['wv'], 512)
        oa = _mla_attn(q, kc, v, mla_past, tq_mla, causal, first_key)

        qd, kd, vd, qi, ki, wi, kst, vst, kist = _dsa_prep(zb3, tabs['dsa_qk'], tabs['idx_q'], tabs['idx_k'], tl, md)
        dsa_past = None
        if past is not None:
            to_heads = lambda c: _pad_cols(jnp.swapaxes(c.astype(md), 1, 2), HEAD_W)
            ones_col = (jnp.arange(HEAD_W) == DH_B).astype(md)
            dsa_past = (to_heads(past['dsa_k'][layer]), to_heads(past['dsa_v'][layer]) + ones_col,
                        past['dsa_kidx'][layer].astype(md))
        ob = _dsa_attn(qd, qi, wi, kd, vd, ki, dsa_past, tq_dsa, causal, first_key, k_sel)

        if past is not None:
            shift, s0 = past['shift'][layer][:, None, :], past['rwkv'][layer]
        else:
            shift, s0 = jnp.zeros((b, 1, C_W), F32), jnp.zeros((b, H_C, N_C, N_C), F32)
        pre = _rwkv_prep(zc3, shift, p, tl, lo, hi)
        oc, s_last = _rwkv_scan(pre, s0, p, c_rwkv, md)

        x1 = _outproj(oa.reshape(t, -1), ob.reshape(t, -1), oc.reshape(t, -1), x2d, p, tm, l, lo, hi, masked)
        dg = _router(x1, p, tm_moe)
        x2d = _moe(x1, dg, p, tm_moe, l, lo, hi, masked)

        n = hi - lo
        new['ckv'].append(ckv_new[:, lo:hi])
        new['krope'].append(kr_new[:, lo:hi])
        new['dsa_k'].append(kst[:, lo:hi].reshape(b, n, H_B, DH_B))
        new['dsa_v'].append(vst[:, lo:hi].reshape(b, n, H_B, DH_B))
        new['dsa_kidx'].append(kist[:, lo:hi])
        new['rwkv'].append(s_last)
        new['shift'].append(zc3[:, hi - 1, :].astype(F32))
    return x2d.reshape(b, l, D_MODEL), {k: jnp.stack(v) for k, v in new.items()}


def kernel(x_prompt, x_sample, cache_mla_ckv, cache_mla_krope, cache_dsa_k, cache_dsa_v, cache_dsa_kidx, state_rwkv, state_rwkv_shift, meta_tokens, w_in, mla_q_norm, mla_kv_norm, mla_w_uq, mla_w_ukv, rwkv_mu, rwkv_w0, rwkv_w_up, rwkv_a0, rwkv_a_up, rwkv_g_up, rwkv_k_k, rwkv_k_a, rwkv_r_k, rwkv_ln_g, rwkv_ln_b, w_out, ln1_g, ln1_b, ln2_g, ln2_b, router_group, router_group_b, router_expert, router_expert_b, exp_w_gate, exp_w_up, exp_w_down):
    weights = {
        'w_in': w_in, 'mla_q_norm': mla_q_norm, 'mla_kv_norm': mla_kv_norm, 'mla_w_uq': mla_w_uq,
        'mla_w_ukv': mla_w_ukv, 'rwkv_mu': rwkv_mu, 'rwkv_w0': rwkv_w0, 'rwkv_w_up': rwkv_w_up,
        'rwkv_a0': rwkv_a0, 'rwkv_a_up': rwkv_a_up, 'rwkv_g_up': rwkv_g_up, 'rwkv_k_k': rwkv_k_k,
        'rwkv_k_a': rwkv_k_a, 'rwkv_r_k': rwkv_r_k, 'rwkv_ln_g': rwkv_ln_g, 'rwkv_ln_b': rwkv_ln_b,
        'w_out': w_out, 'ln1_g': ln1_g, 'ln1_b': ln1_b, 'ln2_g': ln2_g, 'ln2_b': ln2_b,
        'router_group': router_group, 'router_group_b': router_group_b, 'router_expert': router_expert,
        'router_expert_b': router_expert_b, 'exp_w_gate': exp_w_gate, 'exp_w_up': exp_w_up,
        'exp_w_down': exp_w_down,
    }
    wp = _prepare_weights(weights, BF16)
    wp_sample = _prepare_weights(weights, F32)

    bp, seq, _ = x_prompt.shape
    lo, hi = FRONT_PAD, FRONT_PAD + N_META + seq
    lp = -(-hi // KEY_TILE) * KEY_TILE
    meta = jnp.broadcast_to(meta_tokens.astype(F32)[None], (bp, N_META, D_MODEL))
    xp = jnp.concatenate([jnp.zeros((bp, lo, D_MODEL), F32), meta, x_prompt,
                          jnp.zeros((bp, lp - hi, D_MODEL), F32)], axis=1)
    pos_p = jnp.arange(lp, dtype=jnp.int32) - lo
    yp, new_p = _trunk(xp, pos_p, None, wp, lo=lo, hi=hi, causal=True, k_sel=min(TOPK_MAX, seq // 4))
    y_prompt = yp[:, lo + N_META:hi]

    n_past, n_new = cache_mla_ckv.shape[2], x_sample.shape[1]
    assert n_past % (TILE_UNROLL * KEY_TILE) == 0 and n_new <= CHUNK and n_new % 8 == 0, (n_past, n_new)
    pos_s = n_past + jnp.arange(n_new, dtype=jnp.int32)
    past_s = {'ckv': cache_mla_ckv, 'krope': cache_mla_krope, 'dsa_k': cache_dsa_k, 'dsa_v': cache_dsa_v,
              'dsa_kidx': cache_dsa_kidx, 'rwkv': state_rwkv, 'shift': state_rwkv_shift}
    y_sample, new_s = _trunk(x_sample, pos_s, past_s, wp_sample, lo=0, hi=n_new, causal=False,
                             k_sel=min(TOPK_MAX, (n_past + n_new) // 4))

    keys = ('ckv', 'krope', 'dsa_k', 'dsa_v', 'dsa_kidx', 'rwkv', 'shift')
    return (y_prompt, y_sample) + tuple(new_p[k] for k in keys) + tuple(new_s[k] for k in keys)
```
